```python
import jax, jax.numpy as jnp
from jax import lax
import numpy as np

D_MODEL = 1024
BATCH = 8
SEQ = 2048
DEPTH = 1
DEC_BATCH = 128
DEC_SEQ = 4
PAST_LEN = 2048
PAGE_SIZE = 128

N_HEADS = 16
HEAD_DIM = D_MODEL // N_HEADS
ATT_WIDTH = N_HEADS * HEAD_DIM
CONV_CH = D_MODEL
CONV_W = 31
D_FF = ((8 * D_MODEL // 3 + 127) // 128) * 128
Q_BLOCK = 128
LN_EPS = 1e-5
ALPHA = (2.0 * DEPTH) ** 0.25
BETA_INIT = (8.0 * DEPTH) ** -0.25
SB_BIAS_INIT = -8.0
IN_COLS = 3 * ATT_WIDTH + 2 * CONV_CH + 2 * D_MODEL
SPLIT_POINTS = (ATT_WIDTH, 2 * ATT_WIDTH, 3 * ATT_WIDTH, 3 * ATT_WIDTH + 2 * CONV_CH)

kernel_name = 'stickbreak_conformer_conv_macaron_deepnorm_step'


def layer_norm(x, g, b):
    xf = x.astype(jnp.float32)
    mu = jnp.mean(xf, axis=-1, keepdims=True)
    var = jnp.mean(jnp.square(xf - mu), axis=-1, keepdims=True)
    y = (xf - mu) * lax.rsqrt(var + LN_EPS) * g.astype(jnp.float32) + b.astype(jnp.float32)
    return y.astype(x.dtype)


def swiglu_half(x, w_up, w_down):
    gate, up = jnp.split(jnp.einsum('btd,df->btf', x, w_up), 2, axis=-1)
    return jnp.einsum('btf,fd->btd', jax.nn.silu(gate) * up, w_down)


def stick_breaking_block(q, k, v, q_pos, k_pos, b_sb):
    z = jnp.einsum('bqhd,bkhd->bhqk', q, k).astype(jnp.float32) * (HEAD_DIM ** -0.5)
    z = z + b_sb.astype(jnp.float32)[None, :, None, None]
    causal = k_pos[None, :] < q_pos[:, None]
    log_keep = jnp.where(causal, jax.nn.log_sigmoid(-z), 0.0)
    after = lax.cumsum(log_keep, axis=3, reverse=True) - log_keep
    a = jnp.where(causal, jnp.exp(jax.nn.log_sigmoid(z) + after), 0.0)
    return jnp.einsum('bhqk,bkhd->bqhd', a.astype(v.dtype), v)


def stick_breaking(q, k, v, q_pos, k_pos, b_sb):
    B, T, H, Dh = q.shape
    bq = Q_BLOCK if T % Q_BLOCK == 0 else T
    nb = T // bq
    qb = q.reshape(B, nb, bq, H, Dh).transpose(1, 0, 2, 3, 4)
    pb = q_pos.reshape(nb, bq)
    ob = lax.map(lambda qp: stick_breaking_block(qp[0], k, v, qp[1], k_pos, b_sb), (qb, pb))
    return ob.transpose(1, 0, 2, 3, 4).reshape(B, T, H, Dh)


def conv_module(conv_in, buf, w_dw, b_dw, ln_g, ln_b, w_proj_b):
    a, g = jnp.split(conv_in, 2, axis=-1)
    glu = a * jax.nn.sigmoid(g)
    ext = jnp.concatenate([buf.astype(glu.dtype), glu], axis=1)
    y = lax.conv_general_dilated(ext, w_dw[:, None, :].astype(ext.dtype), window_strides=(1,),
                                 padding='VALID', dimension_numbers=('NWC', 'WIO', 'NWC'),
                                 feature_group_count=CONV_CH) + b_dw
    y = jax.nn.silu(layer_norm(y, ln_g, ln_b))
    return jnp.einsum('btc,cd->btd', y, w_proj_b), ext[:, ext.shape[1] - (CONV_W - 1):]


def decoder_layer(x, k_past, v_past, conv_buf, p):
    B, T, _ = x.shape
    past = k_past.shape[1]
    x = layer_norm(ALPHA * x + 0.5 * swiglu_half(x, p['w_ff1_up'], p['w_ff1_down']), p['ln1_g'], p['ln1_b'])
    h = jnp.einsum('btd,de->bte', x, p['w_in'])
    q, k, v, conv_in, gates = jnp.split(h, SPLIT_POINTS, axis=-1)
    q = q.reshape(B, T, N_HEADS, HEAD_DIM)
    k = k.reshape(B, T, N_HEADS, HEAD_DIM)
    v = v.reshape(B, T, N_HEADS, HEAD_DIM)
    k_all = jnp.concatenate([k_past.astype(k.dtype), k], axis=1)
    v_all = jnp.concatenate([v_past.astype(v.dtype), v], axis=1)
    k_pos = jnp.arange(past + T, dtype=jnp.int32)
    q_pos = past + jnp.arange(T, dtype=jnp.int32)
    o_a = stick_breaking(q, k_all, v_all, q_pos, k_pos, p['b_sb']).reshape(B, T, ATT_WIDTH)
    y_a = jnp.einsum('bte,ed->btd', o_a, p['w_proj_a'])
    y_b, new_buf = conv_module(conv_in, conv_buf, p['w_dw'], p['b_dw'], p['ln_conv_g'], p['ln_conv_b'], p['w_proj_b'])
    g_a, g_b = jnp.split(jax.nn.sigmoid(gates + p['b_gate']), 2, axis=-1)
    mix = jnp.einsum('btd,de->bte', g_a * y_a + g_b * y_b, p['w_out'])
    x = layer_norm(ALPHA * x + mix, p['ln2_g'], p['ln2_b'])
    x = layer_norm(ALPHA * x + 0.5 * swiglu_half(x, p['w_ff2_up'], p['w_ff2_down']), p['ln3_g'], p['ln3_b'])
    return x, k, v, new_buf


def setup_inputs(seed: int = 0) -> dict:
    key = jax.random.key(seed)
    ks = jax.random.split(key, 32)
    n_pages = PAST_LEN // PAGE_SIZE
    n_used = DEC_BATCH * n_pages
    n_pool = (n_used * 5) // 4

    def nrm(k, shape, scale):
        return jax.random.normal(k, shape, jnp.float32) * scale

    w_in = nrm(ks[6], (DEPTH, D_MODEL, IN_COLS), D_MODEL ** -0.5)
    w_in = w_in.at[:, :, 2 * ATT_WIDTH:3 * ATT_WIDTH].multiply(BETA_INIT)
    page_table = jax.random.permutation(ks[5], n_pool)[:n_used].reshape(DEC_BATCH, n_pages).astype(jnp.int32)
    return {
        'x_prompt': nrm(ks[0], (BATCH, SEQ, D_MODEL), 1.0),
        'x_sample': nrm(ks[1], (DEC_BATCH, DEC_SEQ, D_MODEL), 1.0),
        'cache_k': nrm(ks[2], (DEPTH, n_pool, PAGE_SIZE, N_HEADS, HEAD_DIM), 1.0),
        'cache_v': nrm(ks[3], (DEPTH, n_pool, PAGE_SIZE, N_HEADS, HEAD_DIM), 1.0),
        'state_conv': nrm(ks[4], (DEPTH, DEC_BATCH, CONV_W - 1, CONV_CH), 0.5),
        'page_table': page_table,
        'w_ff1_up': nrm(ks[7], (DEPTH, D_MODEL, 2 * D_FF), D_MODEL ** -0.5),
        'w_ff1_down': nrm(ks[8], (DEPTH, D_FF, D_MODEL), BETA_INIT * D_FF ** -0.5),
        'ln1_g': 1.0 + nrm(ks[9], (DEPTH, D_MODEL), 0.05),
        'ln1_b': nrm(ks[10], (DEPTH, D_MODEL), 0.02),
        'w_in': w_in,
        'b_sb': SB_BIAS_INIT + nrm(ks[25], (DEPTH, N_HEADS), 0.5),
        'b_gate': nrm(ks[11], (DEPTH, 2 * D_MODEL), 0.1),
        'w_dw': nrm(ks[12], (DEPTH, CONV_W, CONV_CH), CONV_W ** -0.5),
        'b_dw': nrm(ks[13], (DEPTH, CONV_CH), 0.02),
        'ln_conv_g': 1.0 + nrm(ks[14], (DEPTH, CONV_CH), 0.05),
        'ln_conv_b': nrm(ks[15], (DEPTH, CONV_CH), 0.02),
        'w_proj_a': nrm(ks[16], (DEPTH, ATT_WIDTH, D_MODEL), BETA_INIT * ATT_WIDTH ** -0.5),
        'w_proj_b': nrm(ks[17], (DEPTH, CONV_CH, D_MODEL), BETA_INIT * CONV_CH ** -0.5),
        'w_out': nrm(ks[18], (DEPTH, D_MODEL, D_MODEL), BETA_INIT * D_MODEL ** -0.5),
        'ln2_g': 1.0 + nrm(ks[19], (DEPTH, D_MODEL), 0.05),
        'ln2_b': nrm(ks[20], (DEPTH, D_MODEL), 0.02),
        'w_ff2_up': nrm(ks[21], (DEPTH, D_MODEL, 2 * D_FF), D_MODEL ** -0.5),
        'w_ff2_down': nrm(ks[22], (DEPTH, D_FF, D_MODEL), BETA_INIT * D_FF ** -0.5),
        'ln3_g': 1.0 + nrm(ks[23], (DEPTH, D_MODEL), 0.05),
        'ln3_b': nrm(ks[24], (DEPTH, D_MODEL), 0.02),
    }


def reference(x_prompt, x_sample, cache_k, cache_v, state_conv, page_table,
              w_ff1_up, w_ff1_down, ln1_g, ln1_b, w_in, b_sb, b_gate, w_dw, b_dw,
              ln_conv_g, ln_conv_b, w_proj_a, w_proj_b, w_out, ln2_g, ln2_b,
              w_ff2_up, w_ff2_down, ln3_g, ln3_b):
    dec_b, n_pages = page_table.shape
    xp, xs = x_prompt, x_sample
    kp_l, vp_l, cp_l, ks_l, vs_l, cs_l = [], [], [], [], [], []
    for l in range(DEPTH):
        p = dict(w_ff1_up=w_ff1_up[l], w_ff1_down=w_ff1_down[l], ln1_g=ln1_g[l], ln1_b=ln1_b[l],
                 w_in=w_in[l], b_sb=b_sb[l], b_gate=b_gate[l], w_dw=w_dw[l], b_dw=b_dw[l],
                 ln_conv_g=ln_conv_g[l], ln_conv_b=ln_conv_b[l], w_proj_a=w_proj_a[l],
                 w_proj_b=w_proj_b[l], w_out=w_out[l], ln2_g=ln2_g[l], ln2_b=ln2_b[l],
                 w_ff2_up=w_ff2_up[l], w_ff2_down=w_ff2_down[l], ln3_g=ln3_g[l], ln3_b=ln3_b[l])
        b = xp.shape[0]
        empty = jnp.zeros((b, 0, N_HEADS, HEAD_DIM), xp.dtype)
        zero_buf = jnp.zeros((b, CONV_W - 1, CONV_CH), xp.dtype)
        xp, kp, vp, cp = decoder_layer(xp, empty, empty, zero_buf, p)
        k_past = cache_k[l][page_table].reshape(dec_b, n_pages * PAGE_SIZE, N_HEADS, HEAD_DIM)
        v_past = cache_v[l][page_table].reshape(dec_b, n_pages * PAGE_SIZE, N_HEADS, HEAD_DIM)
        xs, kn, vn, cn = decoder_layer(xs, k_past, v_past, state_conv[l], p)
        kp_l.append(kp); vp_l.append(vp); cp_l.append(cp)
        ks_l.append(kn); vs_l.append(vn); cs_l.append(cn)
    return (xp, xs, jnp.stack(kp_l), jnp.stack(vp_l), jnp.stack(cp_l),
            jnp.stack(ks_l), jnp.stack(vs_l), jnp.stack(cs_l))
```

```python
import functools
import math

import jax
import jax.numpy as jnp
from jax import lax
from jax.experimental import pallas as pl
from jax.experimental.pallas import tpu as pltpu

F32 = jnp.float32
BF16 = jnp.bfloat16

LN_EPS = 1e-5
VMEM_LIMIT_BYTES = 56 * 1024 * 1024
LANES = 128
ATT_BLOCK = 256
CONV_HIST_PAD = 32


def _params(*sem):
    return pltpu.CompilerParams(dimension_semantics=sem, vmem_limit_bytes=VMEM_LIMIT_BYTES)


def _layer_norm(y, g, b):
    mu = jnp.mean(y, axis=-1, keepdims=True)
    d = y - mu
    var = jnp.mean(d * d, axis=-1, keepdims=True)
    return d * lax.rsqrt(var + LN_EPS) * g + b


def _sigmoid(x):
    return 1.0 / (1.0 + jnp.exp(-x))


def _softplus(z):
    return jnp.maximum(z, 0.0) + jnp.log(1.0 + jnp.exp(-jnp.abs(z)))


def _dot(a, b):
    return jnp.dot(a, b, preferred_element_type=F32)


def _dot_nt(a, b):
    return lax.dot_general(a, b, (((1,), (1,)), ((), ())), preferred_element_type=F32)


def _split_bf16(x):
    hi = x.astype(BF16)
    lo = (x - hi.astype(F32)).astype(BF16)
    return hi, lo


def _ffn_ln_kernel(x_ref, wg_ref, wu_ref, wd_ref, g_ref, b_ref, o_ref, acc_ref, *, alpha):
    j = pl.program_id(1)
    x = x_ref[...]
    xb = x.astype(BF16)
    hg = _dot(xb, wg_ref[...])
    hu = _dot(xb, wu_ref[...])
    act = (hg * _sigmoid(hg) * hu).astype(BF16)
    part = _dot(act, wd_ref[...])

    @pl.when(j == 0)
    def _():
        acc_ref[...] = part

    @pl.when(j > 0)
    def _():
        acc_ref[...] += part

    @pl.when(j == pl.num_programs(1) - 1)
    def _():
        y = alpha * x + 0.5 * acc_ref[...]
        o_ref[...] = _layer_norm(y, g_ref[...], b_ref[...])


def _ffn_ln(x, w_up, w_down, g, b, *, alpha, tm, n_chunks):
    n, d = x.shape
    f = w_down.shape[0]
    fc = f // n_chunks
    assert f % n_chunks == 0 and fc % LANES == 0 and n % tm == 0
    return pl.pallas_call(
        functools.partial(_ffn_ln_kernel, alpha=alpha),
        grid=(n // tm, n_chunks),
        in_specs=[
            pl.BlockSpec((tm, d), lambda i, j: (i, 0)),
            pl.BlockSpec((d, fc), lambda i, j: (0, j)),
            pl.BlockSpec((d, fc), lambda i, j: (0, j + n_chunks)),
            pl.BlockSpec((fc, d), lambda i, j: (j, 0)),
            pl.BlockSpec((1, d), lambda i, j: (0, 0)),
            pl.BlockSpec((1, d), lambda i, j: (0, 0)),
        ],
        out_specs=pl.BlockSpec((tm, d), lambda i, j: (i, 0)),
        out_shape=jax.ShapeDtypeStruct((n, d), F32),
        scratch_shapes=[pltpu.VMEM((tm, d), F32)],
        compiler_params=_params("parallel", "arbitrary"),
        name="ffn_ln",
    )(x, w_up, w_up, w_down, g, b)


def _qkv_rows_kernel(x_ref, w_ref, q_ref, k_ref, v_ref, *, q_scale):
    d = x_ref.shape[1]
    h = _dot(x_ref[...].astype(BF16), w_ref[...])
    q_ref[...] = h[:, :d] * q_scale
    k_ref[...] = h[:, d:2 * d]
    v_ref[...] = h[:, 2 * d:]


def _qkv_rows(x, w, *, q_scale, tm):
    n, d = x.shape
    row = pl.BlockSpec((tm, d), lambda i: (i, 0))
    return pl.pallas_call(
        functools.partial(_qkv_rows_kernel, q_scale=q_scale),
        grid=(n // tm,),
        in_specs=[row, pl.BlockSpec((d, 3 * d), lambda i: (0, 0))],
        out_specs=[row] * 3,
        out_shape=[jax.ShapeDtypeStruct((n, d), F32)] * 3,
        compiler_params=_params("parallel"),
        name="qkv_rows",
    )(x, w)


def _qkv_cols_kernel(x_ref, wq_ref, wkt_ref, wvt_ref, q_ref, kt_ref, vt_ref, ktb_ref, vtb_ref, *, q_scale):
    xb = x_ref[...].astype(BF16)
    q_ref[...] = (_dot(xb, wq_ref[...]) * q_scale).astype(BF16)
    kt = _dot_nt(wkt_ref[...], xb)
    vt = _dot_nt(wvt_ref[...], xb)
    kt_ref[0] = kt
    vt_ref[0] = vt
    ktb_ref[0, 0] = kt.astype(BF16)
    vtb_ref[0, 0] = vt.astype(BF16)


def _qkv_cols(x, wq, wkt, wvt, *, batch, q_scale, tm):
    n, d = x.shape
    t = n // batch
    nt = t // tm
    mat = pl.BlockSpec((d, d), lambda b, i: (0, 0))
    return pl.pallas_call(
        functools.partial(_qkv_cols_kernel, q_scale=q_scale),
        grid=(batch, nt),
        in_specs=[pl.BlockSpec((tm, d), lambda b, i: (b * nt + i, 0)), mat, mat, mat],
        out_specs=[
            pl.BlockSpec((tm, d), lambda b, i: (b * nt + i, 0)),
            pl.BlockSpec((1, d, tm), lambda b, i: (b, 0, i)),
            pl.BlockSpec((1, d, tm), lambda b, i: (b, 0, i)),
            pl.BlockSpec((1, 1, d, tm), lambda b, i: (b, i, 0, 0)),
            pl.BlockSpec((1, 1, d, tm), lambda b, i: (b, i, 0, 0)),
        ],
        out_shape=[
            jax.ShapeDtypeStruct((n, d), BF16),
            jax.ShapeDtypeStruct((batch, d, t), F32),
            jax.ShapeDtypeStruct((batch, d, t), F32),
            jax.ShapeDtypeStruct((batch, nt, d, tm), BF16),
            jax.ShapeDtypeStruct((batch, nt, d, tm), BF16),
        ],
        compiler_params=_params("parallel", "parallel"),
        name="qkv_cols",
    )(x, wq, wkt, wvt)


def _glu_gate_kernel(x_ref, wc_ref, wg_ref, bg_ref, glu_ref, ga_ref, gb_ref):
    d = x_ref.shape[1]
    xb = x_ref[...].astype(BF16)
    c = _dot(xb, wc_ref[...])
    glu_ref[...] = c[:, :d] * _sigmoid(c[:, d:])
    gates = _sigmoid(_dot(xb, wg_ref[...]) + bg_ref[...])
    ga_ref[...] = gates[:, :d]
    gb_ref[...] = gates[:, d:]


def _glu_gate_proj(x, w_conv, w_gate, b_gate, *, tm):
    n, d = x.shape
    row = pl.BlockSpec((tm, d), lambda i: (i, 0))
    wide = pl.BlockSpec((d, 2 * d), lambda i: (0, 0))
    return pl.pallas_call(
        _glu_gate_kernel,
        grid=(n // tm,),
        in_specs=[row, wide, wide, pl.BlockSpec((1, 2 * d), lambda i: (0, 0))],
        out_specs=[row] * 3,
        out_shape=[jax.ShapeDtypeStruct((n, d), F32)] * 3,
        compiler_params=_params("parallel"),
        name="glu_gate_proj",
    )(x, w_conv, w_gate, b_gate)


def _suffix_sum(sp, tri):
    hi, lo = _split_bf16(sp)
    return _dot(hi, tri) + _dot(lo, tri)


def _attn_prompt_kernel(bias_ref, q_ref, kt_ref, vt_ref, tri_ref, o_ref, *, head_dim):
    blk = ATT_BLOCK
    n_blocks = kt_ref.shape[1]
    pair = pl.program_id(1)
    heads_per_block = LANES // head_dim
    lane = lax.broadcasted_iota(jnp.int32, (1, LANES), 1)
    row = lax.broadcasted_iota(jnp.int32, (blk, blk), 0)
    col = lax.broadcasted_iota(jnp.int32, (blk, blk), 1)
    causal = col < row
    tri = tri_ref[...]

    def q_block_body(qi, _):
        q0 = pl.multiple_of(qi * blk, blk)
        q = q_ref[0, pl.ds(q0, blk), :]
        out = jnp.zeros((blk, LANES), F32)
        for hh in range(heads_per_block):
            in_head = (lane // head_dim) == hh
            bias = bias_ref[pair * heads_per_block + hh]
            qh = jnp.where(in_head, q, jnp.zeros_like(q))

            z = _dot(qh, kt_ref[0, qi]) + bias
            sp = _softplus(z)
            spm = jnp.where(causal, sp, 0.0)
            after = _suffix_sum(spm, tri)
            p = jnp.where(causal, jnp.exp(z - sp - after), 0.0)
            acc = _dot_nt(p.astype(BF16), vt_ref[0, qi])
            carry = jnp.sum(spm, axis=-1, keepdims=True)

            def k_block_body(i, state):
                acc, carry = state
                kj = qi - 1 - i
                z = _dot(qh, kt_ref[0, kj]) + bias
                sp = _softplus(z)
                after = _suffix_sum(sp, tri)
                p = jnp.exp(z - sp - after)
                pv = _dot_nt(p.astype(BF16), vt_ref[0, kj])
                acc = acc + jnp.exp(-carry) * pv
                carry = carry + jnp.sum(sp, axis=-1, keepdims=True)
                return acc, carry

            acc, carry = lax.fori_loop(0, qi, k_block_body, (acc, carry))
            out = jnp.where(in_head, acc, out)
        o_ref[0, pl.ds(q0, blk), :] = out.astype(o_ref.dtype)
        return 0

    lax.fori_loop(0, n_blocks, q_block_body, 0)


def _attn_prompt(q, ktb, vtb, b_sb, *, head_dim):
    b, t, d = q.shape
    blk = ATT_BLOCK
    n_blocks = t // blk
    assert ktb.shape == (b, n_blocks, d, blk) and d % LANES == 0 and LANES % head_dim == 0
    tri = (jnp.arange(blk)[:, None] > jnp.arange(blk)[None, :]).astype(BF16)
    seq = pl.BlockSpec((1, t, LANES), lambda i, p, bias: (i, 0, p))
    seq_t = pl.BlockSpec((1, n_blocks, LANES, blk), lambda i, p, bias: (i, 0, p, 0))
    grid_spec = pltpu.PrefetchScalarGridSpec(
        num_scalar_prefetch=1,
        grid=(b, d // LANES),
        in_specs=[seq, seq_t, seq_t, pl.BlockSpec((blk, blk), lambda i, p, bias: (0, 0))],
        out_specs=seq,
    )
    return pl.pallas_call(
        functools.partial(_attn_prompt_kernel, head_dim=head_dim),
        grid_spec=grid_spec,
        out_shape=jax.ShapeDtypeStruct((b, t, d), BF16),
        compiler_params=_params("parallel", "parallel"),
        name="attn_prompt",
    )(b_sb, q, ktb, vtb, tri)


def _attn_sample_kernel(pt_ref, bias_ref, tri_ref, sel_ref, q_ref, kn_ref, vn_ref, *refs,
                        n_pages, n_heads, head_dim):
    k_refs = refs[:n_pages]
    v_refs = refs[n_pages:2 * n_pages]
    o_ref, knp_ref, vnp_ref = refs[2 * n_pages:]
    tq, d = q_ref.shape[1], q_ref.shape[2]
    page = k_refs[0].shape[2]
    rows = bias_ref.shape[0]
    bias = bias_ref[...]
    tri2 = tri_ref[...]

    q = q_ref[0]
    r_idx = lax.broadcasted_iota(jnp.int32, (rows, d), 0)
    l_idx = lax.broadcasted_iota(jnp.int32, (rows, d), 1)
    q_rows = jnp.concatenate(
        [jnp.broadcast_to(q[i:i + 1, :], (n_heads, d)) for i in range(tq)]
        + [jnp.zeros((rows - tq * n_heads, d), F32)], axis=0)
    wq = jnp.where((l_idx // head_dim) == (r_idx % n_heads), q_rows, 0.0).astype(BF16)

    knp_ref[...] = jnp.zeros_like(knp_ref)
    vnp_ref[...] = jnp.zeros_like(vnp_ref)
    knp_ref[0:tq, :] = kn_ref[0]
    vnp_ref[0:tq, :] = vn_ref[0]
    knt = knp_ref[...].T
    vnt = vnp_ref[...].T
    i_idx = lax.broadcasted_iota(jnp.int32, (rows, page), 0) // n_heads
    j_idx = lax.broadcasted_iota(jnp.int32, (rows, page), 1)
    visible = j_idx < i_idx

    acc = jnp.zeros((d, rows), F32)
    carry = jnp.zeros((rows, 1), F32)
    for pg in range(n_pages, -1, -1):
        is_new = pg == n_pages
        kt = (knt if is_new else k_refs[pg][0]).astype(BF16)
        vt = (vnt if is_new else v_refs[pg][0]).astype(BF16)
        z = _dot(wq, kt) + bias
        sp = _softplus(z)
        spm = jnp.where(visible, sp, 0.0) if is_new else sp
        hi, lo = _split_bf16(spm)
        after = _dot(jnp.concatenate([hi, lo], axis=1), tri2)
        p = jnp.exp(z - sp - after - carry)
        if is_new:
            p = jnp.where(visible, p, 0.0)
        acc = acc + _dot_nt(vt, p.astype(BF16))
        carry = carry + jnp.sum(spm, axis=-1, keepdims=True)

    hr = lax.broadcasted_iota(jnp.int32, (d, rows), 0) // head_dim
    hc = lax.broadcasted_iota(jnp.int32, (d, rows), 1) % n_heads
    own = jnp.where(hr == hc, acc, 0.0).astype(BF16)
    o_ref[0] = _dot_nt(sel_ref[...], own)[0:tq, :]


def _attn_sample(q, k_new, v_new, cache_kt, cache_vt, page_table, b_sb, *, n_heads, head_dim):
    nb, tq, d = q.shape
    n_pages = page_table.shape[1]
    page = cache_kt.shape[2]
    used = tq * n_heads
    rows = -(-used // LANES) * LANES
    r = jnp.arange(rows)
    bias = jnp.where(r < used, b_sb.astype(F32)[r % n_heads], 0.0)
    bias = jnp.broadcast_to(bias[:, None], (rows, page))
    tri = (jnp.arange(page)[:, None] > jnp.arange(page)[None, :]).astype(BF16)
    tri2 = jnp.concatenate([tri, tri], axis=0)
    sel = ((r[None, :] // n_heads == jnp.arange(16)[:, None]) & (r[None, :] < used)).astype(BF16)
    new = pl.BlockSpec((1, tq, d), lambda i, pt: (i, 0, 0))

    def const(a):
        return pl.BlockSpec(a.shape, lambda i, pt: (0,) * a.ndim)

    def page_spec(pg):
        return pl.BlockSpec((1, d, page), lambda i, pt: (pt[i * n_pages + pg], 0, 0))

    grid_spec = pltpu.PrefetchScalarGridSpec(
        num_scalar_prefetch=1,
        grid=(nb,),
        in_specs=[const(bias), const(tri2), const(sel), new, new, new]
        + [page_spec(pg) for pg in range(n_pages)] * 2,
        out_specs=new,
        scratch_shapes=[pltpu.VMEM((page, d), F32)] * 2,
    )
    return pl.pallas_call(
        functools.partial(_attn_sample_kernel, n_pages=n_pages, n_heads=n_heads, head_dim=head_dim),
        grid_spec=grid_spec,
        out_shape=jax.ShapeDtypeStruct((nb, tq, d), F32),
        compiler_params=_params("parallel"),
        name="attn_sample",
    )(page_table.reshape(-1), bias, tri2, sel, q, k_new, v_new, *([cache_kt] * n_pages), *([cache_vt] * n_pages))


def _conv_kernel(hist_ref, cur_ref, w_ref, b_ref, g_ref, be_ref, o_ref, ext_ref, *, width, first_has_no_history):
    nb, tt, c = cur_ref.shape
    pad = ext_ref.shape[1] - tt
    n_hist = width - 1
    hist = hist_ref[...]
    if first_has_no_history:
        hist = jnp.where(pl.program_id(1) == 0, 0.0, hist)
    ext_ref[:, pad - n_hist:pad, :] = hist[:, hist.shape[1] - n_hist:, :]
    ext_ref[:, pad:, :] = cur_ref[...]
    rows = min(tt, 64)
    for bi in range(nb):
        for r0 in range(0, tt, rows):
            cols = []
            for c0 in range(0, c, LANES):
                acc = jnp.zeros((rows, LANES), F32)
                for w in range(width):
                    start = pad - n_hist + r0 + w
                    acc = acc + ext_ref[bi, start:start + rows, c0:c0 + LANES] * w_ref[w:w + 1, c0:c0 + LANES]
                cols.append(acc)
            y = jnp.concatenate(cols, axis=1) + b_ref[...]
            y = _layer_norm(y, g_ref[...], be_ref[...])
            o_ref[bi, r0:r0 + rows, :] = (y * _sigmoid(y)).astype(o_ref.dtype)


def _conv_branch(glu, hist, w_dw, b_dw, g, be, *, tt, nb, first_has_no_history, out_dtype):
    b, t, c = glu.shape
    width = w_dw.shape[0]
    th = hist.shape[1]
    if first_has_no_history:
        hist_rows = CONV_HIST_PAD
        hist_spec = pl.BlockSpec((nb, hist_rows, c), lambda i, j: (i, jnp.maximum(j * (tt // hist_rows) - 1, 0), 0))
    else:
        hist_spec = pl.BlockSpec((nb, th, c), lambda i, j: (i, 0, 0))
    vec = pl.BlockSpec((1, c), lambda i, j: (0, 0))
    return pl.pallas_call(
        functools.partial(_conv_kernel, width=width, first_has_no_history=first_has_no_history),
        grid=(b // nb, t // tt),
        in_specs=[hist_spec, pl.BlockSpec((nb, tt, c), lambda i, j: (i, j, 0)),
                  pl.BlockSpec((width, c), lambda i, j: (0, 0)), vec, vec, vec],
        out_specs=pl.BlockSpec((nb, tt, c), lambda i, j: (i, j, 0)),
        out_shape=jax.ShapeDtypeStruct((b, t, c), out_dtype),
        scratch_shapes=[pltpu.VMEM((nb, CONV_HIST_PAD + tt, c), F32)],
        compiler_params=_params("parallel", "parallel"),
        name="conv_branch",
    )(hist, glu, w_dw, b_dw, g, be)


def _mix_kernel(x_ref, oa_ref, cb_ref, ga_ref, gb_ref, wa_ref, wb_ref, wo_ref, g_ref, b_ref, o_ref, *, alpha):
    ya = _dot(oa_ref[...].astype(BF16), wa_ref[...])
    yb = _dot(cb_ref[...].astype(BF16), wb_ref[...])
    m = (ga_ref[...] * ya + gb_ref[...] * yb).astype(BF16)
    y = alpha * x_ref[...] + _dot(m, wo_ref[...])
    o_ref[...] = _layer_norm(y, g_ref[...], b_ref[...])


def _mix(x1, oa, cb, ga, gb, wa, wb, wo, g, b, *, alpha, tm):
    n, d = x1.shape
    row = pl.BlockSpec((tm, d), lambda i: (i, 0))
    mat = pl.BlockSpec((d, d), lambda i: (0, 0))
    vec = pl.BlockSpec((1, d), lambda i: (0, 0))
    return pl.pallas_call(
        functools.partial(_mix_kernel, alpha=alpha),
        grid=(n // tm,),
        in_specs=[row] * 5 + [mat] * 3 + [vec] * 2,
        out_specs=row,
        out_shape=jax.ShapeDtypeStruct((n, d), F32),
        compiler_params=_params("parallel"),
        name="mix",
    )(x1, oa, cb, ga, gb, wa, wb, wo, g, b)


def kernel(x_prompt, x_sample, cache_k, cache_v, state_conv, page_table, w_ff1_up, w_ff1_down, ln1_g, ln1_b, w_in, b_sb, b_gate, w_dw, b_dw, ln_conv_g, ln_conv_b, w_proj_a, w_proj_b, w_out, ln2_g, ln2_b, w_ff2_up, w_ff2_down, ln3_g, ln3_b):
    depth = w_in.shape[0]
    assert depth == 1, "single-layer step"
    bp, tp, d = x_prompt.shape
    bs, ts, _ = x_sample.shape
    n_heads = b_sb.shape[1]
    head_dim = d // n_heads
    n_pool, page = cache_k.shape[1], cache_k.shape[2]
    alpha = (2.0 * depth) ** 0.25
    q_scale = head_dim ** -0.5
    assert math.frexp(q_scale)[0] == 0.5, "scaling q before its bf16 rounding must be exact"
    n_hist = w_dw.shape[1] - 1

    def vec(a):
        return a[0].reshape(1, -1).astype(F32)

    w1u, w1d = w_ff1_up[0].astype(BF16), w_ff1_down[0].astype(BF16)
    w2u, w2d = w_ff2_up[0].astype(BF16), w_ff2_down[0].astype(BF16)
    w_qkv = w_in[0, :, :3 * d].astype(BF16)
    w_conv = w_in[0, :, 3 * d:5 * d].astype(BF16)
    w_gate = w_in[0, :, 5 * d:].astype(BF16)
    wa, wb, wo = w_proj_a[0].astype(BF16), w_proj_b[0].astype(BF16), w_out[0].astype(BF16)
    ln1, ln2, ln3 = (vec(ln1_g), vec(ln1_b)), (vec(ln2_g), vec(ln2_b)), (vec(ln3_g), vec(ln3_b))
    conv_args = (w_dw[0].astype(F32), vec(b_dw), vec(ln_conv_g), vec(ln_conv_b))

    n = bp * tp
    tm = 512
    x1 = _ffn_ln(x_prompt.reshape(n, d), w1u, w1d, *ln1, alpha=alpha, tm=tm, n_chunks=2)
    q, kt, vt, ktb, vtb = _qkv_cols(x1, w_qkv[:, :d], w_qkv[:, d:2 * d].T, w_qkv[:, 2 * d:].T,
                                    batch=bp, q_scale=q_scale, tm=ATT_BLOCK)
    glu, ga, gb = _glu_gate_proj(x1, w_conv, w_gate, vec(b_gate), tm=256)
    oa = _attn_prompt(q.reshape(bp, tp, d), ktb, vtb, b_sb[0].astype(F32), head_dim=head_dim)
    glu = glu.reshape(bp, tp, d)
    cb = _conv_branch(glu, glu, *conv_args, tt=256, nb=1, first_has_no_history=True, out_dtype=BF16)
    x2 = _mix(x1, oa.reshape(n, d), cb.reshape(n, d), ga, gb, wa, wb, wo, *ln2, alpha=alpha, tm=tm)
    y_prompt = _ffn_ln(x2, w2u, w2d, *ln3, alpha=alpha, tm=tm, n_chunks=2).reshape(bp, tp, d)
    k_prompt = kt.reshape(bp, n_heads, head_dim, tp).transpose(0, 3, 1, 2)[None]
    v_prompt = vt.reshape(bp, n_heads, head_dim, tp).transpose(0, 3, 1, 2)[None]
    conv_prompt = glu[:, tp - n_hist:, :][None]

    n = bs * ts
    x1 = _ffn_ln(x_sample.reshape(n, d), w1u, w1d, *ln1, alpha=alpha, tm=n, n_chunks=2)
    q, k, v = _qkv_rows(x1, w_qkv, q_scale=q_scale, tm=256)
    glu, ga, gb = _glu_gate_proj(x1, w_conv, w_gate, vec(b_gate), tm=256)
    ckt = cache_k[0].transpose(0, 2, 3, 1).reshape(n_pool, d, page)
    cvt = cache_v[0].transpose(0, 2, 3, 1).reshape(n_pool, d, page)
    oa = _attn_sample(q.reshape(bs, ts, d), k.reshape(bs, ts, d), v.reshape(bs, ts, d), ckt, cvt, page_table,
                      b_sb[0], n_heads=n_heads, head_dim=head_dim)
    glu = glu.reshape(bs, ts, d)
    cb = _conv_branch(glu, state_conv[0], *conv_args, tt=ts, nb=8, first_has_no_history=False, out_dtype=F32)
    x2 = _mix(x1, oa.reshape(n, d), cb.reshape(n, d), ga, gb, wa, wb, wo, *ln2, alpha=alpha, tm=n)
    y_sample = _ffn_ln(x2, w2u, w2d, *ln3, alpha=alpha, tm=n, n_chunks=2).reshape(bs, ts, d)
    k_sample = k.reshape(1, bs, ts, n_heads, head_dim)
    v_sample = v.reshape(1, bs, ts, n_heads, head_dim)
    conv_sample = jnp.concatenate([state_conv[0], glu], axis=1)[:, ts:, :][None]

    return y_prompt, y_sample, k_prompt, v_prompt, conv_prompt, k_sample, v_sample, conv_sample
```

```python
import functools
import math

import jax
import jax.numpy as jnp
from jax import lax
from jax.experimental import pallas as pl
from jax.experimental.pallas import tpu as pltpu

F32 = jnp.float32
BF16 = jnp.bfloat16

LN_EPS = 1e-5
LOG2E = math.log2(math.e)
MASKED_SCORE = -1e30
VMEM_LIMIT_BYTES = 56 * 1024 * 1024
LANES = 128
SUBLANES = 8
BF16_ROWS = 16
ATT_BLOCK = 256
CONV_HIST_PAD = 32


def _params(*sem):
    return pltpu.CompilerParams(dimension_semantics=sem, vmem_limit_bytes=VMEM_LIMIT_BYTES)


def _layer_norm(y, g, b):
    mu = jnp.mean(y, axis=-1, keepdims=True)
    d = y - mu
    var = jnp.mean(d * d, axis=-1, keepdims=True)
    return d * lax.rsqrt(var + LN_EPS) * g + b


def _sigmoid(x):
    return 1.0 / (1.0 + jnp.exp(-x))


def _softplus(z):
    return jnp.maximum(z, 0.0) + jnp.log(1.0 + jnp.exp(-jnp.abs(z)))


def _dot(a, b):
    return jnp.dot(a, b, preferred_element_type=F32)


def _dot_nt(a, b):
    return lax.dot_general(a, b, (((1,), (1,)), ((), ())), preferred_element_type=F32)


def _split_bf16(x):
    hi = x.astype(BF16)
    lo = (x - hi.astype(F32)).astype(BF16)
    return hi, lo


def _ffn_ln_kernel(x_ref, wg_ref, wu_ref, wd_ref, g_ref, b_ref, o_ref, acc_ref, *, alpha):
    j = pl.program_id(1)
    x = x_ref[...]
    xb = x.astype(BF16)
    hg = _dot(xb, wg_ref[...])
    hu = _dot(xb, wu_ref[...])
    act = (hg * _sigmoid(hg) * hu).astype(BF16)
    part = _dot(act, wd_ref[...])

    @pl.when(j == 0)
    def _():
        acc_ref[...] = part

    @pl.when(j > 0)
    def _():
        acc_ref[...] += part

    @pl.when(j == pl.num_programs(1) - 1)
    def _():
        y = alpha * x + 0.5 * acc_ref[...]
        o_ref[...] = _layer_norm(y, g_ref[...], b_ref[...])


def _ffn_ln(x, w_up, w_down, g, b, *, alpha, tm, n_chunks):
    n, d = x.shape
    f = w_down.shape[0]
    fc = f // n_chunks
    assert f % n_chunks == 0 and fc % LANES == 0 and n % tm == 0
    return pl.pallas_call(
        functools.partial(_ffn_ln_kernel, alpha=alpha),
        grid=(n // tm, n_chunks),
        in_specs=[
            pl.BlockSpec((tm, d), lambda i, j: (i, 0)),
            pl.BlockSpec((d, fc), lambda i, j: (0, j)),
            pl.BlockSpec((d, fc), lambda i, j: (0, j + n_chunks)),
            pl.BlockSpec((fc, d), lambda i, j: (j, 0)),
            pl.BlockSpec((1, d), lambda i, j: (0, 0)),
            pl.BlockSpec((1, d), lambda i, j: (0, 0)),
        ],
        out_specs=pl.BlockSpec((tm, d), lambda i, j: (i, 0)),
        out_shape=jax.ShapeDtypeStruct((n, d), F32),
        scratch_shapes=[pltpu.VMEM((tm, d), F32)],
        compiler_params=_params("parallel", "arbitrary"),
        name="ffn_ln",
    )(x, w_up, w_up, w_down, g, b)


def _qkv_rows_kernel(x_ref, w_ref, q_ref, k_ref, v_ref, *, q_scale):
    d = x_ref.shape[1]
    h = _dot(x_ref[...].astype(BF16), w_ref[...])
    q_ref[...] = h[:, :d] * q_scale
    k_ref[...] = h[:, d:2 * d]
    v_ref[...] = h[:, 2 * d:]


def _qkv_rows(x, w, *, q_scale, tm):
    n, d = x.shape
    row = pl.BlockSpec((tm, d), lambda i: (i, 0))
    return pl.pallas_call(
        functools.partial(_qkv_rows_kernel, q_scale=q_scale),
        grid=(n // tm,),
        in_specs=[row, pl.BlockSpec((d, 3 * d), lambda i: (0, 0))],
        out_specs=[row] * 3,
        out_shape=[jax.ShapeDtypeStruct((n, d), F32)] * 3,
        compiler_params=_params("parallel"),
        name="qkv_rows",
    )(x, w)


def _qkv_cols_kernel(x_ref, wq_ref, wkt_ref, wvt_ref, q_ref, kt_ref, vt_ref, ktb_ref, vtb_ref, *, q_scale):
    xb = x_ref[...].astype(BF16)
    q_ref[...] = (_dot(xb, wq_ref[...]) * q_scale).astype(BF16)
    kt = _dot_nt(wkt_ref[...], xb)
    vt = _dot_nt(wvt_ref[...], xb)
    kt_ref[0] = kt
    vt_ref[0] = vt
    ktb_ref[0, 0] = kt.astype(BF16)
    vtb_ref[0, 0] = vt.astype(BF16)


def _qkv_cols(x, wq, wkt, wvt, *, batch, q_scale, tm):
    n, d = x.shape
    t = n // batch
    nt = t // tm
    mat = pl.BlockSpec((d, d), lambda b, i: (0, 0))
    return pl.pallas_call(
        functools.partial(_qkv_cols_kernel, q_scale=q_scale),
        grid=(batch, nt),
        in_specs=[pl.BlockSpec((tm, d), lambda b, i: (b * nt + i, 0)), mat, mat, mat],
        out_specs=[
            pl.BlockSpec((tm, d), lambda b, i: (b * nt + i, 0)),
            pl.BlockSpec((1, d, tm), lambda b, i: (b, 0, i)),
            pl.BlockSpec((1, d, tm), lambda b, i: (b, 0, i)),
            pl.BlockSpec((1, 1, d, tm), lambda b, i: (b, i, 0, 0)),
            pl.BlockSpec((1, 1, d, tm), lambda b, i: (b, i, 0, 0)),
        ],
        out_shape=[
            jax.ShapeDtypeStruct((n, d), BF16),
            jax.ShapeDtypeStruct((batch, d, t), F32),
            jax.ShapeDtypeStruct((batch, d, t), F32),
            jax.ShapeDtypeStruct((batch, nt, d, tm), BF16),
            jax.ShapeDtypeStruct((batch, nt, d, tm), BF16),
        ],
        compiler_params=_params("parallel", "parallel"),
        name="qkv_cols",
    )(x, wq, wkt, wvt)


def _glu_gate_kernel(x_ref, wc_ref, wg_ref, bg_ref, glu_ref, ga_ref, gb_ref):
    d = x_ref.shape[1]
    xb = x_ref[...].astype(BF16)
    c = _dot(xb, wc_ref[...])
    glu_ref[...] = c[:, :d] * _sigmoid(c[:, d:])
    gates = _sigmoid(_dot(xb, wg_ref[...]) + bg_ref[...])
    ga_ref[...] = gates[:, :d]
    gb_ref[...] = gates[:, d:]


def _glu_gate_proj(x, w_conv, w_gate, b_gate, *, tm):
    n, d = x.shape
    row = pl.BlockSpec((tm, d), lambda i: (i, 0))
    wide = pl.BlockSpec((d, 2 * d), lambda i: (0, 0))
    return pl.pallas_call(
        _glu_gate_kernel,
        grid=(n // tm,),
        in_specs=[row, wide, wide, pl.BlockSpec((1, 2 * d), lambda i: (0, 0))],
        out_specs=[row] * 3,
        out_shape=[jax.ShapeDtypeStruct((n, d), F32)] * 3,
        compiler_params=_params("parallel"),
        name="glu_gate_proj",
    )(x, w_conv, w_gate, b_gate)


def _softplus2(y):
    neg_abs = lax.bitcast_convert_type(lax.bitcast_convert_type(y, jnp.uint32) | jnp.uint32(1 << 31), F32)
    return jnp.maximum(y, 0.0) + jnp.log2(1.0 + jnp.exp2(neg_abs))


def _attn_prompt_kernel(bias_ref, q_ref, kt_ref, vt_ref, tri_ref, o_ref,
                        y0_ref, h0_ref, y1_ref, h1_ref, acc_ref, carry_ref, *, head_dim):
    blk = ATT_BLOCK
    qblk = 2 * blk
    n_qblocks = q_ref.shape[1] // qblk
    pair = pl.program_id(1)
    heads = range(LANES // head_dim)
    lane = lax.broadcasted_iota(jnp.int32, (1, LANES), 1)
    row = lax.broadcasted_iota(jnp.int32, (qblk, blk), 0)
    col = lax.broadcasted_iota(jnp.int32, (qblk, blk), 1)
    causal = col < row
    tri2 = tri_ref[...]
    in_head = [(lane // head_dim) == hh for hh in heads]
    bias2 = [bias_ref[pair * len(heads) + hh] * LOG2E for hh in heads]
    tile0 = (y0_ref, h0_ref)
    tile1 = (y1_ref, h1_ref)

    def scores(qh, kt, mask):
        ys = [_dot(qh[hh], kt) * LOG2E + bias2[hh] for hh in heads]
        if mask is not None:
            ys = [jnp.where(mask, y, MASKED_SCORE) for y in ys]
        sps = [_softplus2(y) for y in ys]
        return ys, [jnp.concatenate(_split_bf16(sp), axis=1) for sp in sps]

    def weighted_values(ys, hilos, vt):
        incls = [_dot(hl, tri2) for hl in hilos]
        ps = [jnp.exp2(y - incl).astype(BF16) for y, incl in zip(ys, incls)]
        return [_dot_nt(p, vt) for p in ps], [incl[:, :1] for incl in incls]

    def scores_to(tile, qh, kt, mask):
        for hh, (y, hl) in enumerate(zip(*scores(qh, kt, mask))):
            tile[0][hh], tile[1][hh] = y, hl

    def accumulate_from(tile, vt):
        pvs, rss = weighted_values([tile[0][hh] for hh in heads], [tile[1][hh] for hh in heads], vt)
        for hh in heads:
            carry = carry_ref[hh]
            acc_ref[hh] = acc_ref[hh] + jnp.exp2(-carry) * pvs[hh]
            carry_ref[hh] = carry + rss[hh]

    def q_block_body(qi, _):
        q0 = pl.multiple_of(qi * qblk, qblk)
        q = q_ref[0, pl.ds(q0, qblk), :]
        qh = [jnp.where(m, q, jnp.zeros_like(q)) for m in in_head]

        kj = 2 * qi + 1
        pvs, rss = weighted_values(*scores([x[blk:] for x in qh], kt_ref[0, kj], causal[:blk]), vt_ref[0, kj])
        for hh in heads:
            acc_ref[hh] = jnp.concatenate([jnp.zeros_like(pvs[hh]), pvs[hh]], axis=0)
            carry_ref[hh] = jnp.concatenate([jnp.zeros_like(rss[hh]), rss[hh]], axis=0)

        scores_to(tile0, qh, kt_ref[0, 2 * qi], causal)

        def k_block_body(i, _):
            kj = 2 * (qi - i)
            scores_to(tile1, qh, kt_ref[0, kj - 1], None)
            accumulate_from(tile0, vt_ref[0, kj])
            scores_to(tile0, qh, kt_ref[0, kj - 2], None)
            accumulate_from(tile1, vt_ref[0, kj - 1])
            return 0

        lax.fori_loop(0, qi, k_block_body, 0)
        accumulate_from(tile0, vt_ref[0, 0])
        out = jnp.zeros((qblk, LANES), F32)
        for hh in heads:
            out = jnp.where(in_head[hh], acc_ref[hh], out)
        o_ref[0, pl.ds(q0, qblk), :] = out.astype(o_ref.dtype)
        return 0

    lax.fori_loop(0, n_qblocks, q_block_body, 0)


def _attn_prompt(q, ktb, vtb, b_sb, *, head_dim):
    b, t, d = q.shape
    blk = ATT_BLOCK
    n_blocks = t // blk
    heads = LANES // head_dim
    assert ktb.shape == (b, n_blocks, d, blk) and d % LANES == 0 and LANES % head_dim == 0 and n_blocks % 2 == 0
    tri = (jnp.arange(blk)[:, None] >= jnp.arange(blk)[None, :]).astype(BF16)
    tri = jnp.concatenate([tri, tri], axis=0)
    seq = pl.BlockSpec((1, t, LANES), lambda i, p, bias: (i, 0, p))
    seq_t = pl.BlockSpec((1, n_blocks, LANES, blk), lambda i, p, bias: (i, 0, p, 0))
    grid_spec = pltpu.PrefetchScalarGridSpec(
        num_scalar_prefetch=1,
        grid=(b, d // LANES),
        in_specs=[seq, seq_t, seq_t, pl.BlockSpec((2 * blk, blk), lambda i, p, bias: (0, 0))],
        out_specs=seq,
        scratch_shapes=[
            pltpu.VMEM((heads, 2 * blk, blk), F32),
            pltpu.VMEM((heads, 2 * blk, 2 * blk), BF16),
        ] * 2 + [
            pltpu.VMEM((heads, 2 * blk, LANES), F32),
            pltpu.VMEM((heads, 2 * blk, 1), F32),
        ],
    )
    return pl.pallas_call(
        functools.partial(_attn_prompt_kernel, head_dim=head_dim),
        grid_spec=grid_spec,
        out_shape=jax.ShapeDtypeStruct((b, t, d), BF16),
        compiler_params=_params("parallel", "parallel"),
        name="attn_prompt",
    )(b_sb, q, ktb, vtb, tri)


def _attn_sample_kernel(pt_ref, bias_ref, tri_ref, sel_ref, q_ref, kn_ref, vn_ref, *refs,
                        n_pages, n_heads, head_dim):
    k_refs = refs[:n_pages]
    v_refs = refs[n_pages:2 * n_pages]
    o_ref, knp_ref, vnp_ref = refs[2 * n_pages:]
    tq, d = q_ref.shape[1], q_ref.shape[2]
    page = k_refs[0].shape[2]
    rows = bias_ref.shape[0]
    bias = bias_ref[...]
    tri2 = tri_ref[...]

    q = q_ref[0]
    r_idx = lax.broadcasted_iota(jnp.int32, (rows, d), 0)
    l_idx = lax.broadcasted_iota(jnp.int32, (rows, d), 1)
    q_rows = jnp.concatenate(
        [jnp.broadcast_to(q[i:i + 1, :], (n_heads, d)) for i in range(tq)]
        + [jnp.zeros((rows - tq * n_heads, d), F32)], axis=0)
    wq = jnp.where((l_idx // head_dim) == (r_idx % n_heads), q_rows, 0.0).astype(BF16)

    knp_ref[...] = jnp.zeros_like(knp_ref)
    vnp_ref[...] = jnp.zeros_like(vnp_ref)
    knp_ref[0:tq, :] = kn_ref[0]
    vnp_ref[0:tq, :] = vn_ref[0]
    knt = knp_ref[...].T
    vnt = vnp_ref[...].T
    i_idx = lax.broadcasted_iota(jnp.int32, (rows, page), 0) // n_heads
    j_idx = lax.broadcasted_iota(jnp.int32, (rows, page), 1)
    visible = j_idx < i_idx

    acc = jnp.zeros((d, rows), F32)
    carry = jnp.zeros((rows, 1), F32)
    for pg in range(n_pages, -1, -1):
        is_new = pg == n_pages
        kt = (knt if is_new else k_refs[pg][0]).astype(BF16)
        vt = (vnt if is_new else v_refs[pg][0]).astype(BF16)
        z = _dot(wq, kt) + bias
        sp = _softplus(z)
        spm = jnp.where(visible, sp, 0.0) if is_new else sp
        hi, lo = _split_bf16(spm)
        after = _dot(jnp.concatenate([hi, lo], axis=1), tri2)
        p = jnp.exp(z - sp - after - carry)
        if is_new:
            p = jnp.where(visible, p, 0.0)
        acc = acc + _dot_nt(vt, p.astype(BF16))
        carry = carry + jnp.sum(spm, axis=-1, keepdims=True)

    hr = lax.broadcasted_iota(jnp.int32, (d, rows), 0) // head_dim
    hc = lax.broadcasted_iota(jnp.int32, (d, rows), 1) % n_heads
    own = jnp.where(hr == hc, acc, 0.0).astype(BF16)
    o_ref[0] = _dot_nt(sel_ref[...], own)[0:tq, :]


def _attn_sample(q, k_new, v_new, cache_kt, cache_vt, page_table, b_sb, *, n_heads, head_dim):
    nb, tq, d = q.shape
    n_pages = page_table.shape[1]
    page = cache_kt.shape[2]
    used = tq * n_heads
    rows = -(-used // LANES) * LANES
    r = jnp.arange(rows)
    bias = jnp.where(r < used, b_sb.astype(F32)[r % n_heads], 0.0)
    bias = jnp.broadcast_to(bias[:, None], (rows, page))
    tri = (jnp.arange(page)[:, None] > jnp.arange(page)[None, :]).astype(BF16)
    tri2 = jnp.concatenate([tri, tri], axis=0)
    sel = ((r[None, :] // n_heads == jnp.arange(BF16_ROWS)[:, None]) & (r[None, :] < used)).astype(BF16)
    new = pl.BlockSpec((1, tq, d), lambda i, pt: (i, 0, 0))

    def const(a):
        return pl.BlockSpec(a.shape, lambda i, pt: (0,) * a.ndim)

    def page_spec(pg):
        return pl.BlockSpec((1, d, page), lambda i, pt: (pt[i * n_pages + pg], 0, 0))

    grid_spec = pltpu.PrefetchScalarGridSpec(
        num_scalar_prefetch=1,
        grid=(nb,),
        in_specs=[const(bias), const(tri2), const(sel), new, new, new]
        + [page_spec(pg) for pg in range(n_pages)] * 2,
        out_specs=new,
        scratch_shapes=[pltpu.VMEM((page, d), F32)] * 2,
    )
    return pl.pallas_call(
        functools.partial(_attn_sample_kernel, n_pages=n_pages, n_heads=n_heads, head_dim=head_dim),
        grid_spec=grid_spec,
        out_shape=jax.ShapeDtypeStruct((nb, tq, d), F32),
        compiler_params=_params("parallel"),
        name="attn_sample",
    )(page_table.reshape(-1), bias, tri2, sel, q, k_new, v_new, *([cache_kt] * n_pages), *([cache_vt] * n_pages))


def _conv_kernel(hist_ref, cur_ref, w_ref, b_ref, g_ref, be_ref, o_ref, ext_ref, *, width, first_has_no_history):
    nb, tt, c = cur_ref.shape
    pad = ext_ref.shape[1] - tt
    n_hist = width - 1
    hist = hist_ref[...]
    if first_has_no_history:
        hist = jnp.where(pl.program_id(1) == 0, 0.0, hist)
    ext_ref[:, pad - n_hist:pad, :] = hist[:, hist.shape[1] - n_hist:, :]
    ext_ref[:, pad:, :] = cur_ref[...]
    rows = min(tt, 64)
    first = pad - n_hist

    def taps(bi, r0, c0):
        if rows % SUBLANES:
            acc = jnp.zeros((rows, LANES), F32)
            for w in range(width):
                start = first + r0 + w
                acc = acc + ext_ref[bi, start:start + rows, c0:c0 + LANES] * w_ref[w:w + 1, c0:c0 + LANES]
            return acc
        e = ext_ref[bi, r0:r0 + rows + pad, c0:c0 + LANES]
        y = None
        for r in range(SUBLANES):
            n = rows if r == 0 else rows + SUBLANES
            u = None
            for o in range(r, pad + 1, SUBLANES):
                if 0 <= o - first < width:
                    term = e[o - r:o - r + n] * w_ref[o - first:o - first + 1, c0:c0 + LANES]
                    u = term if u is None else u + term
            part = u[r:r + rows]
            y = part if y is None else y + part
        return y

    for bi in range(nb):
        for r0 in range(0, tt, rows):
            cols = [taps(bi, r0, c0) for c0 in range(0, c, LANES)]
            y = jnp.concatenate(cols, axis=1) + b_ref[...]
            y = _layer_norm(y, g_ref[...], be_ref[...])
            o_ref[bi, r0:r0 + rows, :] = (y * _sigmoid(y)).astype(o_ref.dtype)


def _conv_branch(glu, hist, w_dw, b_dw, g, be, *, tt, nb, first_has_no_history, out_dtype):
    b, t, c = glu.shape
    width = w_dw.shape[0]
    th = hist.shape[1]
    if first_has_no_history:
        hist_rows = CONV_HIST_PAD
        hist_spec = pl.BlockSpec((nb, hist_rows, c), lambda i, j: (i, jnp.maximum(j * (tt // hist_rows) - 1, 0), 0))
    else:
        hist_spec = pl.BlockSpec((nb, th, c), lambda i, j: (i, 0, 0))
    vec = pl.BlockSpec((1, c), lambda i, j: (0, 0))
    return pl.pallas_call(
        functools.partial(_conv_kernel, width=width, first_has_no_history=first_has_no_history),
        grid=(b // nb, t // tt),
        in_specs=[hist_spec, pl.BlockSpec((nb, tt, c), lambda i, j: (i, j, 0)),
                  pl.BlockSpec((width, c), lambda i, j: (0, 0)), vec, vec, vec],
        out_specs=pl.BlockSpec((nb, tt, c), lambda i, j: (i, j, 0)),
        out_shape=jax.ShapeDtypeStruct((b, t, c), out_dtype),
        scratch_shapes=[pltpu.VMEM((nb, CONV_HIST_PAD + tt, c), F32)],
        compiler_params=_params("parallel", "parallel"),
        name="conv_branch",
    )(hist, glu, w_dw, b_dw, g, be)


def _mix_kernel(x_ref, oa_ref, cb_ref, ga_ref, gb_ref, wa_ref, wb_ref, wo_ref, g_ref, b_ref, o_ref, *, alpha):
    ya = _dot(oa_ref[...].astype(BF16), wa_ref[...])
    yb = _dot(cb_ref[...].astype(BF16), wb_ref[...])
    m = (ga_ref[...] * ya + gb_ref[...] * yb).astype(BF16)
    y = alpha * x_ref[...] + _dot(m, wo_ref[...])
    o_ref[...] = _layer_norm(y, g_ref[...], b_ref[...])


def _mix(x1, oa, cb, ga, gb, wa, wb, wo, g, b, *, alpha, tm):
    n, d = x1.shape
    row = pl.BlockSpec((tm, d), lambda i: (i, 0))
    mat = pl.BlockSpec((d, d), lambda i: (0, 0))
    vec = pl.BlockSpec((1, d), lambda i: (0, 0))
    return pl.pallas_call(
        functools.partial(_mix_kernel, alpha=alpha),
        grid=(n // tm,),
        in_specs=[row] * 5 + [mat] * 3 + [vec] * 2,
        out_specs=row,
        out_shape=jax.ShapeDtypeStruct((n, d), F32),
        compiler_params=_params("parallel"),
        name="mix",
    )(x1, oa, cb, ga, gb, wa, wb, wo, g, b)


def kernel(x_prompt, x_sample, cache_k, cache_v, state_conv, page_table, w_ff1_up, w_ff1_down, ln1_g, ln1_b, w_in, b_sb, b_gate, w_dw, b_dw, ln_conv_g, ln_conv_b, w_proj_a, w_proj_b, w_out, ln2_g, ln2_b, w_ff2_up, w_ff2_down, ln3_g, ln3_b):
    depth = w_in.shape[0]
    assert depth == 1, "single-layer step"
    bp, tp, d = x_prompt.shape
    bs, ts, _ = x_sample.shape
    n_heads = b_sb.shape[1]
    head_dim = d // n_heads
    n_pool, page = cache_k.shape[1], cache_k.shape[2]
    alpha = (2.0 * depth) ** 0.25
    q_scale = head_dim ** -0.5
    assert math.frexp(q_scale)[0] == 0.5, "scaling q before its bf16 rounding must be exact"
    n_hist = w_dw.shape[1] - 1

    def vec(a):
        return a[0].reshape(1, -1).astype(F32)

    w1u, w1d = w_ff1_up[0].astype(BF16), w_ff1_down[0].astype(BF16)
    w2u, w2d = w_ff2_up[0].astype(BF16), w_ff2_down[0].astype(BF16)
    w_qkv = w_in[0, :, :3 * d].astype(BF16)
    w_conv = w_in[0, :, 3 * d:5 * d].astype(BF16)
    w_gate = w_in[0, :, 5 * d:].astype(BF16)
    wa, wb, wo = w_proj_a[0].astype(BF16), w_proj_b[0].astype(BF16), w_out[0].astype(BF16)
    ln1, ln2, ln3 = (vec(ln1_g), vec(ln1_b)), (vec(ln2_g), vec(ln2_b)), (vec(ln3_g), vec(ln3_b))
    conv_args = (w_dw[0].astype(F32), vec(b_dw), vec(ln_conv_g), vec(ln_conv_b))

    n = bp * tp
    tm = 512
    x1 = _ffn_ln(x_prompt.reshape(n, d), w1u, w1d, *ln1, alpha=alpha, tm=tm, n_chunks=2)
    q, kt, vt, ktb, vtb = _qkv_cols(x1, w_qkv[:, :d], w_qkv[:, d:2 * d].T, w_qkv[:, 2 * d:].T,
                                    batch=bp, q_scale=q_scale, tm=ATT_BLOCK)
    glu, ga, gb = _glu_gate_proj(x1, w_conv, w_gate, vec(b_gate), tm=256)
    oa = _attn_prompt(q.reshape(bp, tp, d), ktb, vtb, b_sb[0].astype(F32), head_dim=head_dim)
    glu = glu.reshape(bp, tp, d)
    cb = _conv_branch(glu, glu, *conv_args, tt=256, nb=1, first_has_no_history=True, out_dtype=BF16)
    x2 = _mix(x1, oa.reshape(n, d), cb.reshape(n, d), ga, gb, wa, wb, wo, *ln2, alpha=alpha, tm=tm)
    y_prompt = _ffn_ln(x2, w2u, w2d, *ln3, alpha=alpha, tm=tm, n_chunks=2).reshape(bp, tp, d)
    k_prompt = kt.reshape(bp, n_heads, head_dim, tp).transpose(0, 3, 1, 2)[None]
    v_prompt = vt.reshape(bp, n_heads, head_dim, tp).transpose(0, 3, 1, 2)[None]
    conv_prompt = glu[:, tp - n_hist:, :][None]

    n = bs * ts
    x1 = _ffn_ln(x_sample.reshape(n, d), w1u, w1d, *ln1, alpha=alpha, tm=n, n_chunks=2)
    q, k, v = _qkv_rows(x1, w_qkv, q_scale=q_scale, tm=256)
    glu, ga, gb = _glu_gate_proj(x1, w_conv, w_gate, vec(b_gate), tm=256)
    ckt = cache_k[0].transpose(0, 2, 3, 1).reshape(n_pool, d, page)
    cvt = cache_v[0].transpose(0, 2, 3, 1).reshape(n_pool, d, page)
    oa = _attn_sample(q.reshape(bs, ts, d), k.reshape(bs, ts, d), v.reshape(bs, ts, d), ckt, cvt, page_table,
                      b_sb[0], n_heads=n_heads, head_dim=head_dim)
    glu = glu.reshape(bs, ts, d)
    cb = _conv_branch(glu, state_conv[0], *conv_args, tt=ts, nb=8, first_has_no_history=False, out_dtype=F32)
    x2 = _mix(x1, oa.reshape(n, d), cb.reshape(n, d), ga, gb, wa, wb, wo, *ln2, alpha=alpha, tm=n)
    y_sample = _ffn_ln(x2, w2u, w2d, *ln3, alpha=alpha, tm=n, n_chunks=2).reshape(bs, ts, d)
    k_sample = k.reshape(1, bs, ts, n_heads, head_dim)
    v_sample = v.reshape(1, bs, ts, n_heads, head_dim)
    conv_sample = jnp.concatenate([state_conv[0], glu], axis=1)[:, ts:, :][None]

    return y_prompt, y_sample, k_prompt, v_prompt, conv_prompt, k_sample, v_sample, conv_sample
```

```python
import functools
import math

import jax
import jax.numpy as jnp
from jax import lax
from jax.experimental import pallas as pl
from jax.experimental.pallas import tpu as pltpu

F32 = jnp.float32
BF16 = jnp.bfloat16

LN_EPS = 1e-5
LOG2E = math.log2(math.e)
MASKED_SCORE = -1e30
VMEM_LIMIT_BYTES = 56 * 1024 * 1024
LANES = 128
SUBLANES = 8
BF16_ROWS = 16
ATT_BLOCK = 256
CONV_HIST_PAD = 32


def _params(*sem):
    return pltpu.CompilerParams(dimension_semantics=sem, vmem_limit_bytes=VMEM_LIMIT_BYTES)


def _layer_norm(y, g, b):
    mu = jnp.mean(y, axis=-1, keepdims=True)
    d = y - mu
    var = jnp.mean(d * d, axis=-1, keepdims=True)
    return d * lax.rsqrt(var + LN_EPS) * g + b


def _sigmoid(x):
    return 1.0 / (1.0 + jnp.exp(-x))


def _softplus(z):
    return jnp.maximum(z, 0.0) + jnp.log(1.0 + jnp.exp(-jnp.abs(z)))


def _dot(a, b):
    return jnp.dot(a, b, preferred_element_type=F32)


def _dot_nt(a, b):
    return lax.dot_general(a, b, (((1,), (1,)), ((), ())), preferred_element_type=F32)


def _split_bf16(x):
    hi = x.astype(BF16)
    lo = (x - hi.astype(F32)).astype(BF16)
    return hi, lo


def _ffn_ln_kernel(x_ref, wg_ref, wu_ref, wd_ref, g_ref, b_ref, o_ref, *, alpha):
    x = x_ref[...]
    xb = x.astype(BF16)
    hg = _dot(xb, wg_ref[...])
    hu = _dot(xb, wu_ref[...])
    act = (hg * _sigmoid(hg) * hu).astype(BF16)
    y = alpha * x + 0.5 * _dot(act, wd_ref[...])
    o_ref[...] = _layer_norm(y, g_ref[...], b_ref[...])


def _resident(shape, index_map):
    return pl.BlockSpec(shape, index_map, pipeline_mode=pl.Buffered(1))


def _ffn_ln(x, w_up, w_down, g, b, *, alpha, tm):
    n, d = x.shape
    f = w_down.shape[0]
    assert f % LANES == 0 and n % tm == 0
    return pl.pallas_call(
        functools.partial(_ffn_ln_kernel, alpha=alpha),
        grid=(n // tm,),
        in_specs=[
            pl.BlockSpec((tm, d), lambda i: (i, 0)),
            _resident((d, f), lambda i: (0, 0)),
            _resident((d, f), lambda i: (0, 1)),
            _resident((f, d), lambda i: (0, 0)),
            pl.BlockSpec((1, d), lambda i: (0, 0)),
            pl.BlockSpec((1, d), lambda i: (0, 0)),
        ],
        out_specs=pl.BlockSpec((tm, d), lambda i: (i, 0)),
        out_shape=jax.ShapeDtypeStruct((n, d), F32),
        compiler_params=_params("parallel"),
        name="ffn_ln",
    )(x, w_up, w_up, w_down, g, b)


def _qkv_rows_kernel(x_ref, w_ref, q_ref, k_ref, v_ref, *, q_scale):
    d = x_ref.shape[1]
    h = _dot(x_ref[...].astype(BF16), w_ref[...])
    q_ref[...] = h[:, :d] * q_scale
    k_ref[...] = h[:, d:2 * d]
    v_ref[...] = h[:, 2 * d:]


def _qkv_rows(x, w, *, q_scale, tm):
    n, d = x.shape
    row = pl.BlockSpec((tm, d), lambda i: (i, 0))
    return pl.pallas_call(
        functools.partial(_qkv_rows_kernel, q_scale=q_scale),
        grid=(n // tm,),
        in_specs=[row, pl.BlockSpec((d, 3 * d), lambda i: (0, 0))],
        out_specs=[row] * 3,
        out_shape=[jax.ShapeDtypeStruct((n, d), F32)] * 3,
        compiler_params=_params("parallel"),
        name="qkv_rows",
    )(x, w)


def _qkv_cols_kernel(x_ref, wq_ref, wkt_ref, wvt_ref, q_ref, kt_ref, vt_ref, ktb_ref, vtb_ref, *, q_scale):
    xb = x_ref[...].astype(BF16)
    q_ref[...] = (_dot(xb, wq_ref[...]) * q_scale).astype(BF16)
    kt = _dot_nt(wkt_ref[...], xb)
    vt = _dot_nt(wvt_ref[...], xb)
    kt_ref[0] = kt
    vt_ref[0] = vt
    ktb_ref[0, 0] = kt.astype(BF16)
    vtb_ref[0, 0] = vt.astype(BF16)


def _qkv_cols(x, wq, wkt, wvt, *, batch, q_scale, tm):
    n, d = x.shape
    t = n // batch
    nt = t // tm
    mat = pl.BlockSpec((d, d), lambda b, i: (0, 0))
    return pl.pallas_call(
        functools.partial(_qkv_cols_kernel, q_scale=q_scale),
        grid=(batch, nt),
        in_specs=[pl.BlockSpec((tm, d), lambda b, i: (b * nt + i, 0)), mat, mat, mat],
        out_specs=[
            pl.BlockSpec((tm, d), lambda b, i: (b * nt + i, 0)),
            pl.BlockSpec((1, d, tm), lambda b, i: (b, 0, i)),
            pl.BlockSpec((1, d, tm), lambda b, i: (b, 0, i)),
            pl.BlockSpec((1, 1, d, tm), lambda b, i: (b, i, 0, 0)),
            pl.BlockSpec((1, 1, d, tm), lambda b, i: (b, i, 0, 0)),
        ],
        out_shape=[
            jax.ShapeDtypeStruct((n, d), BF16),
            jax.ShapeDtypeStruct((batch, d, t), F32),
            jax.ShapeDtypeStruct((batch, d, t), F32),
            jax.ShapeDtypeStruct((batch, nt, d, tm), BF16),
            jax.ShapeDtypeStruct((batch, nt, d, tm), BF16),
        ],
        compiler_params=_params("parallel", "parallel"),
        name="qkv_cols",
    )(x, wq, wkt, wvt)


def _glu_gate_kernel(x_ref, wc_ref, wg_ref, bg_ref, glu_ref, ga_ref, gb_ref):
    d = x_ref.shape[1]
    xb = x_ref[...].astype(BF16)
    c = _dot(xb, wc_ref[...])
    glu_ref[...] = c[:, :d] * _sigmoid(c[:, d:])
    gates = _sigmoid(_dot(xb, wg_ref[...]) + bg_ref[...])
    ga_ref[...] = gates[:, :d]
    gb_ref[...] = gates[:, d:]


def _glu_gate_proj(x, w_conv, w_gate, b_gate, *, tm):
    n, d = x.shape
    row = pl.BlockSpec((tm, d), lambda i: (i, 0))
    wide = pl.BlockSpec((d, 2 * d), lambda i: (0, 0))
    return pl.pallas_call(
        _glu_gate_kernel,
        grid=(n // tm,),
        in_specs=[row, wide, wide, pl.BlockSpec((1, 2 * d), lambda i: (0, 0))],
        out_specs=[row] * 3,
        out_shape=[jax.ShapeDtypeStruct((n, d), F32)] * 3,
        compiler_params=_params("parallel"),
        name="glu_gate_proj",
    )(x, w_conv, w_gate, b_gate)


def _softplus2(y):
    neg_abs = lax.bitcast_convert_type(lax.bitcast_convert_type(y, jnp.uint32) | jnp.uint32(1 << 31), F32)
    return jnp.maximum(y, 0.0) + jnp.log2(1.0 + jnp.exp2(neg_abs))


def _attn_prompt_kernel(bias_ref, q_ref, kt_ref, vt_ref, tri_ref, o_ref,
                        y0_ref, h0_ref, y1_ref, h1_ref, acc_ref, carry_ref, *, head_dim):
    blk = ATT_BLOCK
    qblk = 2 * blk
    n_qblocks = q_ref.shape[1] // qblk
    pair = pl.program_id(1)
    heads = range(LANES // head_dim)
    lane = lax.broadcasted_iota(jnp.int32, (1, LANES), 1)
    row = lax.broadcasted_iota(jnp.int32, (qblk, blk), 0)
    col = lax.broadcasted_iota(jnp.int32, (qblk, blk), 1)
    causal = col < row
    tri2 = tri_ref[...]
    in_head = [(lane // head_dim) == hh for hh in heads]
    bias2 = [bias_ref[pair * len(heads) + hh] * LOG2E for hh in heads]
    tile0 = (y0_ref, h0_ref)
    tile1 = (y1_ref, h1_ref)

    def scores(qh, kt, mask):
        ys = [_dot(qh[hh], kt) * LOG2E + bias2[hh] for hh in heads]
        if mask is not None:
            ys = [jnp.where(mask, y, MASKED_SCORE) for y in ys]
        sps = [_softplus2(y) for y in ys]
        return ys, [jnp.concatenate(_split_bf16(sp), axis=1) for sp in sps]

    def weighted_values(ys, hilos, vt):
        incls = [_dot(hl, tri2) for hl in hilos]
        ps = [jnp.exp2(y - incl).astype(BF16) for y, incl in zip(ys, incls)]
        return [_dot_nt(p, vt) for p in ps], [incl[:, :1] for incl in incls]

    def scores_to(tile, qh, kt, mask):
        for hh, (y, hl) in enumerate(zip(*scores(qh, kt, mask))):
            tile[0][hh], tile[1][hh] = y, hl

    def accumulate_from(tile, vt):
        pvs, rss = weighted_values([tile[0][hh] for hh in heads], [tile[1][hh] for hh in heads], vt)
        for hh in heads:
            carry = carry_ref[hh]
            acc_ref[hh] = acc_ref[hh] + jnp.exp2(-carry) * pvs[hh]
            carry_ref[hh] = carry + rss[hh]

    def q_block_body(qi, _):
        q0 = pl.multiple_of(qi * qblk, qblk)
        q = q_ref[0, pl.ds(q0, qblk), :]
        qh = [jnp.where(m, q, jnp.zeros_like(q)) for m in in_head]

        kj = 2 * qi + 1
        pvs, rss = weighted_values(*scores([x[blk:] for x in qh], kt_ref[0, kj], causal[:blk]), vt_ref[0, kj])
        for hh in heads:
            acc_ref[hh] = jnp.concatenate([jnp.zeros_like(pvs[hh]), pvs[hh]], axis=0)
            carry_ref[hh] = jnp.concatenate([jnp.zeros_like(rss[hh]), rss[hh]], axis=0)

        scores_to(tile0, qh, kt_ref[0, 2 * qi], causal)

        def k_block_body(i, _):
            kj = 2 * (qi - i)
            scores_to(tile1, qh, kt_ref[0, kj - 1], None)
            accumulate_from(tile0, vt_ref[0, kj])
            scores_to(tile0, qh, kt_ref[0, kj - 2], None)
            accumulate_from(tile1, vt_ref[0, kj - 1])
            return 0

        lax.fori_loop(0, qi, k_block_body, 0)
        accumulate_from(tile0, vt_ref[0, 0])
        out = jnp.zeros((qblk, LANES), F32)
        for hh in heads:
            out = jnp.where(in_head[hh], acc_ref[hh], out)
        o_ref[0, pl.ds(q0, qblk), :] = out.astype(o_ref.dtype)
        return 0

    lax.fori_loop(0, n_qblocks, q_block_body, 0)


def _attn_prompt(q, ktb, vtb, b_sb, *, head_dim):
    b, t, d = q.shape
    blk = ATT_BLOCK
    n_blocks = t // blk
    heads = LANES // head_dim
    assert ktb.shape == (b, n_blocks, d, blk) and d % LANES == 0 and LANES % head_dim == 0 and n_blocks % 2 == 0
    tri = (jnp.arange(blk)[:, None] >= jnp.arange(blk)[None, :]).astype(BF16)
    tri = jnp.concatenate([tri, tri], axis=0)
    seq = pl.BlockSpec((1, t, LANES), lambda i, p, bias: (i, 0, p))
    seq_t = pl.BlockSpec((1, n_blocks, LANES, blk), lambda i, p, bias: (i, 0, p, 0))
    grid_spec = pltpu.PrefetchScalarGridSpec(
        num_scalar_prefetch=1,
        grid=(b, d // LANES),
        in_specs=[seq, seq_t, seq_t, pl.BlockSpec((2 * blk, blk), lambda i, p, bias: (0, 0))],
        out_specs=seq,
        scratch_shapes=[
            pltpu.VMEM((heads, 2 * blk, blk), F32),
            pltpu.VMEM((heads, 2 * blk, 2 * blk), BF16),
        ] * 2 + [
            pltpu.VMEM((heads, 2 * blk, LANES), F32),
            pltpu.VMEM((heads, 2 * blk, 1), F32),
        ],
    )
    return pl.pallas_call(
        functools.partial(_attn_prompt_kernel, head_dim=head_dim),
        grid_spec=grid_spec,
        out_shape=jax.ShapeDtypeStruct((b, t, d), BF16),
        compiler_params=_params("parallel", "parallel"),
        name="attn_prompt",
    )(b_sb, q, ktb, vtb, tri)


def _attn_sample_kernel(pt_ref, bias_ref, tri_ref, sel_ref, q_ref, kn_ref, vn_ref, *refs,
                        n_pages, n_heads, head_dim):
    k_refs = refs[:n_pages]
    v_refs = refs[n_pages:2 * n_pages]
    o_ref, knp_ref, vnp_ref = refs[2 * n_pages:]
    tq, d = q_ref.shape[1], q_ref.shape[2]
    page = k_refs[0].shape[2]
    rows = bias_ref.shape[0]
    bias = bias_ref[...]
    tri2 = tri_ref[...]

    q = q_ref[0]
    r_idx = lax.broadcasted_iota(jnp.int32, (rows, d), 0)
    l_idx = lax.broadcasted_iota(jnp.int32, (rows, d), 1)
    q_rows = jnp.concatenate(
        [jnp.broadcast_to(q[i:i + 1, :], (n_heads, d)) for i in range(tq)]
        + [jnp.zeros((rows - tq * n_heads, d), F32)], axis=0)
    wq = jnp.where((l_idx // head_dim) == (r_idx % n_heads), q_rows, 0.0).astype(BF16)

    knp_ref[...] = jnp.zeros_like(knp_ref)
    vnp_ref[...] = jnp.zeros_like(vnp_ref)
    knp_ref[0:tq, :] = kn_ref[0]
    vnp_ref[0:tq, :] = vn_ref[0]
    knt = knp_ref[...].T
    vnt = vnp_ref[...].T
    i_idx = lax.broadcasted_iota(jnp.int32, (rows, page), 0) // n_heads
    j_idx = lax.broadcasted_iota(jnp.int32, (rows, page), 1)
    visible = j_idx < i_idx

    def scores(pg, carry):
        is_new = pg == n_pages
        kt = (knt if is_new else k_refs[pg][0]).astype(BF16)
        z = _dot(wq, kt) + bias
        sp = _softplus(z)
        spm = jnp.where(visible, sp, 0.0) if is_new else sp
        hilo = jnp.concatenate(_split_bf16(spm), axis=1)
        return (z - sp - carry, hilo), carry + jnp.sum(spm, axis=-1, keepdims=True)

    def weighted_values(pg, tile):
        is_new = pg == n_pages
        vt = (vnt if is_new else v_refs[pg][0]).astype(BF16)
        p = jnp.exp(tile[0] - _dot(tile[1], tri2))
        if is_new:
            p = jnp.where(visible, p, 0.0)
        return _dot_nt(vt, p.astype(BF16))

    acc = jnp.zeros((d, rows), F32)
    tile, carry = scores(n_pages, jnp.zeros((rows, 1), F32))
    for pg in range(n_pages, -1, -1):
        if pg > 0:
            nxt, carry = scores(pg - 1, carry)
        acc = acc + weighted_values(pg, tile)
        tile = nxt

    hr = lax.broadcasted_iota(jnp.int32, (d, rows), 0) // head_dim
    hc = lax.broadcasted_iota(jnp.int32, (d, rows), 1) % n_heads
    own = jnp.where(hr == hc, acc, 0.0).astype(BF16)
    o_ref[0] = _dot_nt(sel_ref[...], own)[0:tq, :]


def _attn_sample(q, k_new, v_new, cache_kt, cache_vt, page_table, b_sb, *, n_heads, head_dim):
    nb, tq, d = q.shape
    n_pages = page_table.shape[1]
    page = cache_kt.shape[2]
    used = tq * n_heads
    rows = -(-used // LANES) * LANES
    r = jnp.arange(rows)
    bias = jnp.where(r < used, b_sb.astype(F32)[r % n_heads], 0.0)
    bias = jnp.broadcast_to(bias[:, None], (rows, page))
    tri = (jnp.arange(page)[:, None] > jnp.arange(page)[None, :]).astype(BF16)
    tri2 = jnp.concatenate([tri, tri], axis=0)
    sel = ((r[None, :] // n_heads == jnp.arange(BF16_ROWS)[:, None]) & (r[None, :] < used)).astype(BF16)
    new = pl.BlockSpec((1, tq, d), lambda i, pt: (i, 0, 0))

    def const(a):
        return pl.BlockSpec(a.shape, lambda i, pt: (0,) * a.ndim)

    def page_spec(pg):
        return pl.BlockSpec((1, d, page), lambda i, pt: (pt[i * n_pages + pg], 0, 0))

    grid_spec = pltpu.PrefetchScalarGridSpec(
        num_scalar_prefetch=1,
        grid=(nb,),
        in_specs=[const(bias), const(tri2), const(sel), new, new, new]
        + [page_spec(pg) for pg in range(n_pages)] * 2,
        out_specs=new,
        scratch_shapes=[pltpu.VMEM((page, d), F32)] * 2,
    )
    return pl.pallas_call(
        functools.partial(_attn_sample_kernel, n_pages=n_pages, n_heads=n_heads, head_dim=head_dim),
        grid_spec=grid_spec,
        out_shape=jax.ShapeDtypeStruct((nb, tq, d), F32),
        compiler_params=_params("parallel"),
        name="attn_sample",
    )(page_table.reshape(-1), bias, tri2, sel, q, k_new, v_new, *([cache_kt] * n_pages), *([cache_vt] * n_pages))


def _conv_kernel(hist_ref, cur_ref, w_ref, b_ref, g_ref, be_ref, o_ref, ext_ref, *, width, first_has_no_history):
    nb, tt, c = cur_ref.shape
    pad = ext_ref.shape[1] - tt
    n_hist = width - 1
    hist = hist_ref[...]
    if first_has_no_history:
        hist = jnp.where(pl.program_id(1) == 0, 0.0, hist)
    ext_ref[:, pad - n_hist:pad, :] = hist[:, hist.shape[1] - n_hist:, :]
    ext_ref[:, pad:, :] = cur_ref[...]
    rows = min(tt, 64)
    first = pad - n_hist

    def taps(bi, r0, c0):
        if rows % SUBLANES:
            acc = jnp.zeros((rows, LANES), F32)
            for w in range(width):
                start = first + r0 + w
                acc = acc + ext_ref[bi, start:start + rows, c0:c0 + LANES] * w_ref[w:w + 1, c0:c0 + LANES]
            return acc
        e = ext_ref[bi, r0:r0 + rows + pad, c0:c0 + LANES]
        y = None
        for r in range(SUBLANES):
            n = rows if r == 0 else rows + SUBLANES
            u = None
            for o in range(r, pad + 1, SUBLANES):
                if 0 <= o - first < width:
                    term = e[o - r:o - r + n] * w_ref[o - first:o - first + 1, c0:c0 + LANES]
                    u = term if u is None else u + term
            part = u[r:r + rows]
            y = part if y is None else y + part
        return y

    for bi in range(nb):
        for r0 in range(0, tt, rows):
            cols = [taps(bi, r0, c0) for c0 in range(0, c, LANES)]
            y = jnp.concatenate(cols, axis=1) + b_ref[...]
            y = _layer_norm(y, g_ref[...], be_ref[...])
            o_ref[bi, r0:r0 + rows, :] = (y * _sigmoid(y)).astype(o_ref.dtype)


def _conv_branch(glu, hist, w_dw, b_dw, g, be, *, tt, nb, first_has_no_history, out_dtype):
    b, t, c = glu.shape
    width = w_dw.shape[0]
    th = hist.shape[1]
    if first_has_no_history:
        hist_rows = CONV_HIST_PAD
        hist_spec = pl.BlockSpec((nb, hist_rows, c), lambda i, j: (i, jnp.maximum(j * (tt // hist_rows) - 1, 0), 0))
    else:
        hist_spec = pl.BlockSpec((nb, th, c), lambda i, j: (i, 0, 0))
    vec = pl.BlockSpec((1, c), lambda i, j: (0, 0))
    return pl.pallas_call(
        functools.partial(_conv_kernel, width=width, first_has_no_history=first_has_no_history),
        grid=(b // nb, t // tt),
        in_specs=[hist_spec, pl.BlockSpec((nb, tt, c), lambda i, j: (i, j, 0)),
                  pl.BlockSpec((width, c), lambda i, j: (0, 0)), vec, vec, vec],
        out_specs=pl.BlockSpec((nb, tt, c), lambda i, j: (i, j, 0)),
        out_shape=jax.ShapeDtypeStruct((b, t, c), out_dtype),
        scratch_shapes=[pltpu.VMEM((nb, CONV_HIST_PAD + tt, c), F32)],
        compiler_params=_params("parallel", "parallel"),
        name="conv_branch",
    )(hist, glu, w_dw, b_dw, g, be)


def _mix_kernel(x_ref, oa_ref, cb_ref, ga_ref, gb_ref, wa_ref, wb_ref, wo_ref, g_ref, b_ref, o_ref, *, alpha):
    ya = _dot(oa_ref[...].astype(BF16), wa_ref[...])
    yb = _dot(cb_ref[...].astype(BF16), wb_ref[...])
    m = (ga_ref[...] * ya + gb_ref[...] * yb).astype(BF16)
    y = alpha * x_ref[...] + _dot(m, wo_ref[...])
    o_ref[...] = _layer_norm(y, g_ref[...], b_ref[...])


def _mix(x1, oa, cb, ga, gb, wa, wb, wo, g, b, *, alpha, tm):
    n, d = x1.shape
    row = pl.BlockSpec((tm, d), lambda i: (i, 0))
    mat = pl.BlockSpec((d, d), lambda i: (0, 0))
    vec = pl.BlockSpec((1, d), lambda i: (0, 0))
    return pl.pallas_call(
        functools.partial(_mix_kernel, alpha=alpha),
        grid=(n // tm,),
        in_specs=[row] * 5 + [mat] * 3 + [vec] * 2,
        out_specs=row,
        out_shape=jax.ShapeDtypeStruct((n, d), F32),
        compiler_params=_params("parallel"),
        name="mix",
    )(x1, oa, cb, ga, gb, wa, wb, wo, g, b)


def kernel(x_prompt, x_sample, cache_k, cache_v, state_conv, page_table, w_ff1_up, w_ff1_down, ln1_g, ln1_b, w_in, b_sb, b_gate, w_dw, b_dw, ln_conv_g, ln_conv_b, w_proj_a, w_proj_b, w_out, ln2_g, ln2_b, w_ff2_up, w_ff2_down, ln3_g, ln3_b):
    depth = w_in.shape[0]
    assert depth == 1, "single-layer step"
    bp, tp, d = x_prompt.shape
    bs, ts, _ = x_sample.shape
    n_heads = b_sb.shape[1]
    head_dim = d // n_heads
    n_pool, page = cache_k.shape[1], cache_k.shape[2]
    alpha = (2.0 * depth) ** 0.25
    q_scale = head_dim ** -0.5
    assert math.frexp(q_scale)[0] == 0.5, "scaling q before its bf16 rounding must be exact"
    n_hist = w_dw.shape[1] - 1

    def vec(a):
        return a[0].reshape(1, -1).astype(F32)

    w1u, w1d = w_ff1_up[0].astype(BF16), w_ff1_down[0].astype(BF16)
    w2u, w2d = w_ff2_up[0].astype(BF16), w_ff2_down[0].astype(BF16)
    w_qkv = w_in[0, :, :3 * d].astype(BF16)
    w_conv = w_in[0, :, 3 * d:5 * d].astype(BF16)
    w_gate = w_in[0, :, 5 * d:].astype(BF16)
    wa, wb, wo = w_proj_a[0].astype(BF16), w_proj_b[0].astype(BF16), w_out[0].astype(BF16)
    ln1, ln2, ln3 = (vec(ln1_g), vec(ln1_b)), (vec(ln2_g), vec(ln2_b)), (vec(ln3_g), vec(ln3_b))
    conv_args = (w_dw[0].astype(F32), vec(b_dw), vec(ln_conv_g), vec(ln_conv_b))

    n = bp * tp
    tm = 512
    x1 = _ffn_ln(x_prompt.reshape(n, d), w1u, w1d, *ln1, alpha=alpha, tm=tm)
    q, kt, vt, ktb, vtb = _qkv_cols(x1, w_qkv[:, :d], w_qkv[:, d:2 * d].T, w_qkv[:, 2 * d:].T,
                                    batch=bp, q_scale=q_scale, tm=ATT_BLOCK)
    glu, ga, gb = _glu_gate_proj(x1, w_conv, w_gate, vec(b_gate), tm=256)
    oa = _attn_prompt(q.reshape(bp, tp, d), ktb, vtb, b_sb[0].astype(F32), head_dim=head_dim)
    glu = glu.reshape(bp, tp, d)
    cb = _conv_branch(glu, glu, *conv_args, tt=256, nb=1, first_has_no_history=True, out_dtype=BF16)
    x2 = _mix(x1, oa.reshape(n, d), cb.reshape(n, d), ga, gb, wa, wb, wo, *ln2, alpha=alpha, tm=tm)
    y_prompt = _ffn_ln(x2, w2u, w2d, *ln3, alpha=alpha, tm=tm).reshape(bp, tp, d)
    k_prompt = kt.reshape(bp, n_heads, head_dim, tp).transpose(0, 3, 1, 2)[None]
    v_prompt = vt.reshape(bp, n_heads, head_dim, tp).transpose(0, 3, 1, 2)[None]
    conv_prompt = glu[:, tp - n_hist:, :][None]

    n = bs * ts
    x1 = _ffn_ln(x_sample.reshape(n, d), w1u, w1d, *ln1, alpha=alpha, tm=n)
    q, k, v = _qkv_rows(x1, w_qkv, q_scale=q_scale, tm=256)
    glu, ga, gb = _glu_gate_proj(x1, w_conv, w_gate, vec(b_gate), tm=256)
    ckt = cache_k[0].transpose(0, 2, 3, 1).reshape(n_pool, d, page)
    cvt = cache_v[0].transpose(0, 2, 3, 1).reshape(n_pool, d, page)
    oa = _attn_sample(q.reshape(bs, ts, d), k.reshape(bs, ts, d), v.reshape(bs, ts, d), ckt, cvt, page_table,
                      b_sb[0], n_heads=n_heads, head_dim=head_dim)
    glu = glu.reshape(bs, ts, d)
    cb = _conv_branch(glu, state_conv[0], *conv_args, tt=ts, nb=8, first_has_no_history=False, out_dtype=F32)
    x2 = _mix(x1, oa.reshape(n, d), cb.reshape(n, d), ga, gb, wa, wb, wo, *ln2, alpha=alpha, tm=n)
    y_sample = _ffn_ln(x2, w2u, w2d, *ln3, alpha=alpha, tm=n).reshape(bs, ts, d)
    k_sample = k.reshape(1, bs, ts, n_heads, head_dim)
    v_sample = v.reshape(1, bs, ts, n_heads, head_dim)
    conv_sample = jnp.concatenate([state_conv[0], glu], axis=1)[:, ts:, :][None]

    return y_prompt, y_sample, k_prompt, v_prompt, conv_prompt, k_sample, v_sample, conv_sample
```

```python
import functools
import math

import jax
import jax.numpy as jnp
from jax import lax
from jax.experimental import pallas as pl
from jax.experimental.pallas import tpu as pltpu

F32 = jnp.float32
BF16 = jnp.bfloat16

LN_EPS = 1e-5
LOG2E = math.log2(math.e)
MASKED_SCORE = -1e30
VMEM_LIMIT_BYTES = 56 * 1024 * 1024
LANES = 128
SUBLANES = 8
BF16_ROWS = 16
ATT_BLOCK = 256
CONV_HIST_PAD = 32


def _params(*sem):
    return pltpu.CompilerParams(dimension_semantics=sem, vmem_limit_bytes=VMEM_LIMIT_BYTES)


def _layer_norm(y, g, b):
    mu = jnp.mean(y, axis=-1, keepdims=True)
    d = y - mu
    var = jnp.mean(d * d, axis=-1, keepdims=True)
    return d * lax.rsqrt(var + LN_EPS) * g + b


def _sigmoid(x):
    return 1.0 / (1.0 + jnp.exp(-x))


def _softplus(z):
    return jnp.maximum(z, 0.0) + jnp.log(1.0 + jnp.exp(-jnp.abs(z)))


def _dot(a, b):
    return jnp.dot(a, b, preferred_element_type=F32)


def _dot_nt(a, b):
    return lax.dot_general(a, b, (((1,), (1,)), ((), ())), preferred_element_type=F32)


def _split_bf16(x):
    hi = x.astype(BF16)
    lo = (x - hi.astype(F32)).astype(BF16)
    return hi, lo


def _ffn_ln_kernel(x_ref, wg_ref, wu_ref, wd_ref, g_ref, b_ref, o_ref, *, alpha):
    x = x_ref[...]
    xb = x.astype(BF16)
    hg = _dot(xb, wg_ref[...])
    hu = _dot(xb, wu_ref[...])
    act = (hg * _sigmoid(hg) * hu).astype(BF16)
    y = alpha * x + 0.5 * _dot(act, wd_ref[...])
    o_ref[...] = _layer_norm(y, g_ref[...], b_ref[...])


def _resident(shape, index_map):
    return pl.BlockSpec(shape, index_map, pipeline_mode=pl.Buffered(1))


def _ffn_ln(x, w_up, w_down, g, b, *, alpha, tm):
    n, d = x.shape
    f = w_down.shape[0]
    assert f % LANES == 0 and n % tm == 0
    return pl.pallas_call(
        functools.partial(_ffn_ln_kernel, alpha=alpha),
        grid=(n // tm,),
        in_specs=[
            pl.BlockSpec((tm, d), lambda i: (i, 0)),
            _resident((d, f), lambda i: (0, 0)),
            _resident((d, f), lambda i: (0, 1)),
            _resident((f, d), lambda i: (0, 0)),
            pl.BlockSpec((1, d), lambda i: (0, 0)),
            pl.BlockSpec((1, d), lambda i: (0, 0)),
        ],
        out_specs=pl.BlockSpec((tm, d), lambda i: (i, 0)),
        out_shape=jax.ShapeDtypeStruct((n, d), F32),
        compiler_params=_params("parallel"),
        name="ffn_ln",
    )(x, w_up, w_up, w_down, g, b)


def _qkv_rows_kernel(x_ref, w_ref, q_ref, k_ref, v_ref, *, q_scale):
    d = x_ref.shape[1]
    h = _dot(x_ref[...].astype(BF16), w_ref[...])
    q_ref[...] = h[:, :d] * q_scale
    k_ref[...] = h[:, d:2 * d]
    v_ref[...] = h[:, 2 * d:]


def _qkv_rows(x, w, *, q_scale, tm):
    n, d = x.shape
    row = pl.BlockSpec((tm, d), lambda i: (i, 0))
    return pl.pallas_call(
        functools.partial(_qkv_rows_kernel, q_scale=q_scale),
        grid=(n // tm,),
        in_specs=[row, pl.BlockSpec((d, 3 * d), lambda i: (0, 0))],
        out_specs=[row] * 3,
        out_shape=[jax.ShapeDtypeStruct((n, d), F32)] * 3,
        compiler_params=_params("parallel"),
        name="qkv_rows",
    )(x, w)


def _qkv_cols_kernel(x_ref, wq_ref, wkt_ref, wvt_ref, q_ref, kt_ref, vt_ref, ktb_ref, vtb_ref, *, q_scale):
    xb = x_ref[...].astype(BF16)
    q_ref[...] = (_dot(xb, wq_ref[...]) * q_scale).astype(BF16)
    kt = _dot_nt(wkt_ref[...], xb)
    vt = _dot_nt(wvt_ref[...], xb)
    kt_ref[0] = kt
    vt_ref[0] = vt
    ktb_ref[0, 0] = kt.astype(BF16)
    vtb_ref[0, 0] = vt.astype(BF16)


def _qkv_cols(x, wq, wkt, wvt, *, batch, q_scale, tm):
    n, d = x.shape
    t = n // batch
    nt = t // tm
    mat = pl.BlockSpec((d, d), lambda b, i: (0, 0))
    return pl.pallas_call(
        functools.partial(_qkv_cols_kernel, q_scale=q_scale),
        grid=(batch, nt),
        in_specs=[pl.BlockSpec((tm, d), lambda b, i: (b * nt + i, 0)), mat, mat, mat],
        out_specs=[
            pl.BlockSpec((tm, d), lambda b, i: (b * nt + i, 0)),
            pl.BlockSpec((1, d, tm), lambda b, i: (b, 0, i)),
            pl.BlockSpec((1, d, tm), lambda b, i: (b, 0, i)),
            pl.BlockSpec((1, 1, d, tm), lambda b, i: (b, i, 0, 0)),
            pl.BlockSpec((1, 1, d, tm), lambda b, i: (b, i, 0, 0)),
        ],
        out_shape=[
            jax.ShapeDtypeStruct((n, d), BF16),
            jax.ShapeDtypeStruct((batch, d, t), F32),
            jax.ShapeDtypeStruct((batch, d, t), F32),
            jax.ShapeDtypeStruct((batch, nt, d, tm), BF16),
            jax.ShapeDtypeStruct((batch, nt, d, tm), BF16),
        ],
        compiler_params=_params("parallel", "parallel"),
        name="qkv_cols",
    )(x, wq, wkt, wvt)


def _glu_gate_kernel(x_ref, wc_ref, wg_ref, bg_ref, glu_ref, ga_ref, gb_ref):
    d = x_ref.shape[1]
    xb = x_ref[...].astype(BF16)
    c = _dot(xb, wc_ref[...])
    glu_ref[...] = c[:, :d] * _sigmoid(c[:, d:])
    gates = _sigmoid(_dot(xb, wg_ref[...]) + bg_ref[...])
    ga_ref[...] = gates[:, :d]
    gb_ref[...] = gates[:, d:]


def _glu_gate_proj(x, w_conv, w_gate, b_gate, *, tm):
    n, d = x.shape
    row = pl.BlockSpec((tm, d), lambda i: (i, 0))
    wide = pl.BlockSpec((d, 2 * d), lambda i: (0, 0))
    return pl.pallas_call(
        _glu_gate_kernel,
        grid=(n // tm,),
        in_specs=[row, wide, wide, pl.BlockSpec((1, 2 * d), lambda i: (0, 0))],
        out_specs=[row] * 3,
        out_shape=[jax.ShapeDtypeStruct((n, d), F32)] * 3,
        compiler_params=_params("parallel"),
        name="glu_gate_proj",
    )(x, w_conv, w_gate, b_gate)


def _softplus2(y):
    neg_abs = lax.bitcast_convert_type(lax.bitcast_convert_type(y, jnp.uint32) | jnp.uint32(1 << 31), F32)
    return jnp.maximum(y, 0.0) + jnp.log2(1.0 + jnp.exp2(neg_abs))


def _attn_prompt_kernel(bias_ref, q_ref, kt_ref, vt_ref, tri_ref, o_ref,
                        y0_ref, h0_ref, y1_ref, h1_ref, acc_ref, carry_ref, *, head_dim):
    blk = ATT_BLOCK
    qblk = 2 * blk
    n_qblocks = q_ref.shape[1] // qblk
    pair = pl.program_id(1)
    heads = range(LANES // head_dim)
    lane = lax.broadcasted_iota(jnp.int32, (1, LANES), 1)
    row = lax.broadcasted_iota(jnp.int32, (qblk, blk), 0)
    col = lax.broadcasted_iota(jnp.int32, (qblk, blk), 1)
    causal = col < row
    tri2 = tri_ref[...]
    in_head = [(lane // head_dim) == hh for hh in heads]
    bias2 = [bias_ref[pair * len(heads) + hh] * LOG2E for hh in heads]
    tile0 = (y0_ref, h0_ref)
    tile1 = (y1_ref, h1_ref)

    def scores(qh, kt, mask):
        ys = [_dot(qh[hh], kt) * LOG2E + bias2[hh] for hh in heads]
        if mask is not None:
            ys = [jnp.where(mask, y, MASKED_SCORE) for y in ys]
        sps = [_softplus2(y) for y in ys]
        return ys, [jnp.concatenate(_split_bf16(sp), axis=1) for sp in sps]

    def weighted_values(ys, hilos, vt):
        incls = [_dot(hl, tri2) for hl in hilos]
        ps = [jnp.exp2(y - incl).astype(BF16) for y, incl in zip(ys, incls)]
        return [_dot_nt(p, vt) for p in ps], [incl[:, :1] for incl in incls]

    def scores_to(tile, qh, kt, mask):
        for hh, (y, hl) in enumerate(zip(*scores(qh, kt, mask))):
            tile[0][hh], tile[1][hh] = y, hl

    def accumulate_from(tile, vt, slot):
        pvs, rss = weighted_values([tile[0][hh] for hh in heads], [tile[1][hh] for hh in heads], vt)
        for hh in heads:
            carry = carry_ref[slot, hh]
            acc_ref[slot, hh] = acc_ref[slot, hh] + jnp.exp2(-carry) * pvs[hh]
            carry_ref[slot, hh] = carry + rss[hh]

    tiles = (tile0, tile1)
    qhs, first = None, None
    for qi in range(n_qblocks + 1):
        slot, cur, other = qi % 2, tiles[qi % 2], tiles[1 - qi % 2]
        prev_qhs = qhs
        if qi < n_qblocks:
            q = q_ref[0, qi * qblk:(qi + 1) * qblk, :]
            qhs = [jnp.where(m, q, jnp.zeros_like(q)) for m in in_head]
            scores_to(cur, qhs, kt_ref[0, 2 * qi], causal)
            first = scores([x[blk:] for x in qhs], kt_ref[0, 2 * qi + 1], causal[:blk])
        if qi > 0:
            accumulate_from(other, vt_ref[0, 0], 1 - slot)
            out = jnp.zeros((qblk, LANES), F32)
            for hh in heads:
                out = jnp.where(in_head[hh], acc_ref[1 - slot, hh], out)
            o_ref[0, (qi - 1) * qblk:qi * qblk, :] = out.astype(o_ref.dtype)
        if qi == n_qblocks:
            break
        pvs, rss = weighted_values(*first, vt_ref[0, 2 * qi + 1])
        for hh in heads:
            acc_ref[slot, hh] = jnp.concatenate([jnp.zeros_like(pvs[hh]), pvs[hh]], axis=0)
            carry_ref[slot, hh] = jnp.concatenate([jnp.zeros_like(rss[hh]), rss[hh]], axis=0)

        def k_block_body(i, _, qi=qi, qhs=qhs, slot=slot, cur=cur, other=other):
            kj = 2 * (qi - i)
            scores_to(other, qhs, kt_ref[0, kj - 1], None)
            accumulate_from(cur, vt_ref[0, kj], slot)
            scores_to(cur, qhs, kt_ref[0, kj - 2], None)
            accumulate_from(other, vt_ref[0, kj - 1], slot)
            return 0

        if qi > 0:
            lax.fori_loop(0, qi, k_block_body, 0)


def _attn_prompt(q, ktb, vtb, b_sb, *, head_dim):
    b, t, d = q.shape
    blk = ATT_BLOCK
    n_blocks = t // blk
    heads = LANES // head_dim
    assert ktb.shape == (b, n_blocks, d, blk) and d % LANES == 0 and LANES % head_dim == 0 and n_blocks % 2 == 0
    tri = (jnp.arange(blk)[:, None] >= jnp.arange(blk)[None, :]).astype(BF16)
    tri = jnp.concatenate([tri, tri], axis=0)
    seq = pl.BlockSpec((1, t, LANES), lambda i, p, bias: (i, 0, p))
    seq_t = pl.BlockSpec((1, n_blocks, LANES, blk), lambda i, p, bias: (i, 0, p, 0))
    grid_spec = pltpu.PrefetchScalarGridSpec(
        num_scalar_prefetch=1,
        grid=(b, d // LANES),
        in_specs=[seq, seq_t, seq_t, pl.BlockSpec((2 * blk, blk), lambda i, p, bias: (0, 0))],
        out_specs=seq,
        scratch_shapes=[
            pltpu.VMEM((heads, 2 * blk, blk), F32),
            pltpu.VMEM((heads, 2 * blk, 2 * blk), BF16),
        ] * 2 + [
            pltpu.VMEM((2, heads, 2 * blk, LANES), F32),
            pltpu.VMEM((2, heads, 2 * blk, 1), F32),
        ],
    )
    return pl.pallas_call(
        functools.partial(_attn_prompt_kernel, head_dim=head_dim),
        grid_spec=grid_spec,
        out_shape=jax.ShapeDtypeStruct((b, t, d), BF16),
        compiler_params=_params("parallel", "parallel"),
        name="attn_prompt",
    )(b_sb, q, ktb, vtb, tri)


def _attn_sample_kernel(pt_ref, bias_ref, tri_ref, sel_ref, q_ref, kn_ref, vn_ref, *refs,
                        n_pages, n_heads, head_dim):
    k_refs = refs[:n_pages]
    v_refs = refs[n_pages:2 * n_pages]
    o_ref, knp_ref, vnp_ref = refs[2 * n_pages:]
    tq, d = q_ref.shape[1], q_ref.shape[2]
    page = k_refs[0].shape[2]
    rows = bias_ref.shape[0]
    bias = bias_ref[...]
    tri2 = tri_ref[...]

    q = q_ref[0]
    r_idx = lax.broadcasted_iota(jnp.int32, (rows, d), 0)
    l_idx = lax.broadcasted_iota(jnp.int32, (rows, d), 1)
    q_rows = jnp.concatenate(
        [jnp.broadcast_to(q[i:i + 1, :], (n_heads, d)) for i in range(tq)]
        + [jnp.zeros((rows - tq * n_heads, d), F32)], axis=0)
    wq = jnp.where((l_idx // head_dim) == (r_idx % n_heads), q_rows, 0.0).astype(BF16)

    knp_ref[...] = jnp.zeros_like(knp_ref)
    vnp_ref[...] = jnp.zeros_like(vnp_ref)
    knp_ref[0:tq, :] = kn_ref[0]
    vnp_ref[0:tq, :] = vn_ref[0]
    knt = knp_ref[...].T
    vnt = vnp_ref[...].T
    i_idx = lax.broadcasted_iota(jnp.int32, (rows, page), 0) // n_heads
    j_idx = lax.broadcasted_iota(jnp.int32, (rows, page), 1)
    visible = j_idx < i_idx

    def scores(pg, carry):
        is_new = pg == n_pages
        kt = (knt if is_new else k_refs[pg][0]).astype(BF16)
        z = _dot(wq, kt) + bias
        sp = _softplus(z)
        spm = jnp.where(visible, sp, 0.0) if is_new else sp
        hilo = jnp.concatenate(_split_bf16(spm), axis=1)
        return (z - sp - carry, hilo), carry + jnp.sum(spm, axis=-1, keepdims=True)

    def weighted_values(pg, tile):
        is_new = pg == n_pages
        vt = (vnt if is_new else v_refs[pg][0]).astype(BF16)
        p = jnp.exp(tile[0] - _dot(tile[1], tri2))
        if is_new:
            p = jnp.where(visible, p, 0.0)
        return _dot_nt(vt, p.astype(BF16))

    acc = jnp.zeros((d, rows), F32)
    tile, carry = scores(n_pages, jnp.zeros((rows, 1), F32))
    for pg in range(n_pages, -1, -1):
        if pg > 0:
            nxt, carry = scores(pg - 1, carry)
        acc = acc + weighted_values(pg, tile)
        tile = nxt

    hr = lax.broadcasted_iota(jnp.int32, (d, rows), 0) // head_dim
    hc = lax.broadcasted_iota(jnp.int32, (d, rows), 1) % n_heads
    own = jnp.where(hr == hc, acc, 0.0).astype(BF16)
    o_ref[0] = _dot_nt(sel_ref[...], own)[0:tq, :]


def _attn_sample(q, k_new, v_new, cache_kt, cache_vt, page_table, b_sb, *, n_heads, head_dim):
    nb, tq, d = q.shape
    n_pages = page_table.shape[1]
    page = cache_kt.shape[2]
    used = tq * n_heads
    rows = -(-used // LANES) * LANES
    r = jnp.arange(rows)
    bias = jnp.where(r < used, b_sb.astype(F32)[r % n_heads], 0.0)
    bias = jnp.broadcast_to(bias[:, None], (rows, page))
    tri = (jnp.arange(page)[:, None] > jnp.arange(page)[None, :]).astype(BF16)
    tri2 = jnp.concatenate([tri, tri], axis=0)
    sel = ((r[None, :] // n_heads == jnp.arange(BF16_ROWS)[:, None]) & (r[None, :] < used)).astype(BF16)
    new = pl.BlockSpec((1, tq, d), lambda i, pt: (i, 0, 0))

    def const(a):
        return pl.BlockSpec(a.shape, lambda i, pt: (0,) * a.ndim)

    def page_spec(pg):
        return pl.BlockSpec((1, d, page), lambda i, pt: (pt[i * n_pages + pg], 0, 0))

    grid_spec = pltpu.PrefetchScalarGridSpec(
        num_scalar_prefetch=1,
        grid=(nb,),
        in_specs=[const(bias), const(tri2), const(sel), new, new, new]
        + [page_spec(pg) for pg in range(n_pages)] * 2,
        out_specs=new,
        scratch_shapes=[pltpu.VMEM((page, d), F32)] * 2,
    )
    return pl.pallas_call(
        functools.partial(_attn_sample_kernel, n_pages=n_pages, n_heads=n_heads, head_dim=head_dim),
        grid_spec=grid_spec,
        out_shape=jax.ShapeDtypeStruct((nb, tq, d), F32),
        compiler_params=_params("parallel"),
        name="attn_sample",
    )(page_table.reshape(-1), bias, tri2, sel, q, k_new, v_new, *([cache_kt] * n_pages), *([cache_vt] * n_pages))


def _conv_kernel(hist_ref, cur_ref, w_ref, b_ref, g_ref, be_ref, o_ref, ext_ref, *, width, first_has_no_history):
    nb, tt, c = cur_ref.shape
    pad = ext_ref.shape[1] - tt
    n_hist = width - 1
    hist = hist_ref[...]
    if first_has_no_history:
        hist = jnp.where(pl.program_id(1) == 0, 0.0, hist)
    ext_ref[:, pad - n_hist:pad, :] = hist[:, hist.shape[1] - n_hist:, :]
    ext_ref[:, pad:, :] = cur_ref[...]
    rows = min(tt, 64)
    first = pad - n_hist

    def taps(bi, r0, c0):
        if rows % SUBLANES:
            acc = jnp.zeros((rows, LANES), F32)
            for w in range(width):
                start = first + r0 + w
                acc = acc + ext_ref[bi, start:start + rows, c0:c0 + LANES] * w_ref[w:w + 1, c0:c0 + LANES]
            return acc
        e = ext_ref[bi, r0:r0 + rows + pad, c0:c0 + LANES]
        y = None
        for r in range(SUBLANES):
            n = rows if r == 0 else rows + SUBLANES
            u = None
            for o in range(r, pad + 1, SUBLANES):
                if 0 <= o - first < width:
                    term = e[o - r:o - r + n] * w_ref[o - first:o - first + 1, c0:c0 + LANES]
                    u = term if u is None else u + term
            part = u[r:r + rows]
            y = part if y is None else y + part
        return y

    for bi in range(nb):
        for r0 in range(0, tt, rows):
            cols = [taps(bi, r0, c0) for c0 in range(0, c, LANES)]
            y = jnp.concatenate(cols, axis=1) + b_ref[...]
            y = _layer_norm(y, g_ref[...], be_ref[...])
            o_ref[bi, r0:r0 + rows, :] = (y * _sigmoid(y)).astype(o_ref.dtype)


def _conv_branch(glu, hist, w_dw, b_dw, g, be, *, tt, nb, first_has_no_history, out_dtype):
    b, t, c = glu.shape
    width = w_dw.shape[0]
    th = hist.shape[1]
    if first_has_no_history:
        hist_rows = CONV_HIST_PAD
        hist_spec = pl.BlockSpec((nb, hist_rows, c), lambda i, j: (i, jnp.maximum(j * (tt // hist_rows) - 1, 0), 0))
    else:
        hist_spec = pl.BlockSpec((nb, th, c), lambda i, j: (i, 0, 0))
    vec = pl.BlockSpec((1, c), lambda i, j: (0, 0))
    return pl.pallas_call(
        functools.partial(_conv_kernel, width=width, first_has_no_history=first_has_no_history),
        grid=(b // nb, t // tt),
        in_specs=[hist_spec, pl.BlockSpec((nb, tt, c), lambda i, j: (i, j, 0)),
                  pl.BlockSpec((width, c), lambda i, j: (0, 0)), vec, vec, vec],
        out_specs=pl.BlockSpec((nb, tt, c), lambda i, j: (i, j, 0)),
        out_shape=jax.ShapeDtypeStruct((b, t, c), out_dtype),
        scratch_shapes=[pltpu.VMEM((nb, CONV_HIST_PAD + tt, c), F32)],
        compiler_params=_params("parallel", "parallel"),
        name="conv_branch",
    )(hist, glu, w_dw, b_dw, g, be)


def _mix_kernel(x_ref, oa_ref, cb_ref, ga_ref, gb_ref, wa_ref, wb_ref, wo_ref, g_ref, b_ref, o_ref, *, alpha):
    ya = _dot(oa_ref[...].astype(BF16), wa_ref[...])
    yb = _dot(cb_ref[...].astype(BF16), wb_ref[...])
    m = (ga_ref[...] * ya + gb_ref[...] * yb).astype(BF16)
    y = alpha * x_ref[...] + _dot(m, wo_ref[...])
    o_ref[...] = _layer_norm(y, g_ref[...], b_ref[...])


def _mix(x1, oa, cb, ga, gb, wa, wb, wo, g, b, *, alpha, tm):
    n, d = x1.shape
    row = pl.BlockSpec((tm, d), lambda i: (i, 0))
    mat = pl.BlockSpec((d, d), lambda i: (0, 0))
    vec = pl.BlockSpec((1, d), lambda i: (0, 0))
    return pl.pallas_call(
        functools.partial(_mix_kernel, alpha=alpha),
        grid=(n // tm,),
        in_specs=[row] * 5 + [mat] * 3 + [vec] * 2,
        out_specs=row,
        out_shape=jax.ShapeDtypeStruct((n, d), F32),
        compiler_params=_params("parallel"),
        name="mix",
    )(x1, oa, cb, ga, gb, wa, wb, wo, g, b)


def kernel(x_prompt, x_sample, cache_k, cache_v, state_conv, page_table, w_ff1_up, w_ff1_down, ln1_g, ln1_b, w_in, b_sb, b_gate, w_dw, b_dw, ln_conv_g, ln_conv_b, w_proj_a, w_proj_b, w_out, ln2_g, ln2_b, w_ff2_up, w_ff2_down, ln3_g, ln3_b):
    depth = w_in.shape[0]
    assert depth == 1, "single-layer step"
    bp, tp, d = x_prompt.shape
    bs, ts, _ = x_sample.shape
    n_heads = b_sb.shape[1]
    head_dim = d // n_heads
    n_pool, page = cache_k.shape[1], cache_k.shape[2]
    alpha = (2.0 * depth) ** 0.25
    q_scale = head_dim ** -0.5
    assert math.frexp(q_scale)[0] == 0.5, "scaling q before its bf16 rounding must be exact"
    n_hist = w_dw.shape[1] - 1

    def vec(a):
        return a[0].reshape(1, -1).astype(F32)

    w1u, w1d = w_ff1_up[0].astype(BF16), w_ff1_down[0].astype(BF16)
    w2u, w2d = w_ff2_up[0].astype(BF16), w_ff2_down[0].astype(BF16)
    w_qkv = w_in[0, :, :3 * d].astype(BF16)
    w_conv = w_in[0, :, 3 * d:5 * d].astype(BF16)
    w_gate = w_in[0, :, 5 * d:].astype(BF16)
    wa, wb, wo = w_proj_a[0].astype(BF16), w_proj_b[0].astype(BF16), w_out[0].astype(BF16)
    ln1, ln2, ln3 = (vec(ln1_g), vec(ln1_b)), (vec(ln2_g), vec(ln2_b)), (vec(ln3_g), vec(ln3_b))
    conv_args = (w_dw[0].astype(F32), vec(b_dw), vec(ln_conv_g), vec(ln_conv_b))

    n = bp * tp
    tm = 512
    x1 = _ffn_ln(x_prompt.reshape(n, d), w1u, w1d, *ln1, alpha=alpha, tm=tm)
    q, kt, vt, ktb, vtb = _qkv_cols(x1, w_qkv[:, :d], w_qkv[:, d:2 * d].T, w_qkv[:, 2 * d:].T,
                                    batch=bp, q_scale=q_scale, tm=ATT_BLOCK)
    glu, ga, gb = _glu_gate_proj(x1, w_conv, w_gate, vec(b_gate), tm=tm)
    oa = _attn_prompt(q.reshape(bp, tp, d), ktb, vtb, b_sb[0].astype(F32), head_dim=head_dim)
    glu = glu.reshape(bp, tp, d)
    cb = _conv_branch(glu, glu, *conv_args, tt=256, nb=1, first_has_no_history=True, out_dtype=BF16)
    x2 = _mix(x1, oa.reshape(n, d), cb.reshape(n, d), ga, gb, wa, wb, wo, *ln2, alpha=alpha, tm=tm)
    y_prompt = _ffn_ln(x2, w2u, w2d, *ln3, alpha=alpha, tm=tm).reshape(bp, tp, d)
    k_prompt = kt.reshape(bp, n_heads, head_dim, tp).transpose(0, 3, 1, 2)[None]
    v_prompt = vt.reshape(bp, n_heads, head_dim, tp).transpose(0, 3, 1, 2)[None]
    conv_prompt = glu[:, tp - n_hist:, :][None]

    n = bs * ts
    x1 = _ffn_ln(x_sample.reshape(n, d), w1u, w1d, *ln1, alpha=alpha, tm=n)
    q, k, v = _qkv_rows(x1, w_qkv, q_scale=q_scale, tm=256)
    glu, ga, gb = _glu_gate_proj(x1, w_conv, w_gate, vec(b_gate), tm=256)
    ckt = cache_k[0].transpose(0, 2, 3, 1).reshape(n_pool, d, page)
    cvt = cache_v[0].transpose(0, 2, 3, 1).reshape(n_pool, d, page)
    oa = _attn_sample(q.reshape(bs, ts, d), k.reshape(bs, ts, d), v.reshape(bs, ts, d), ckt, cvt, page_table,
                      b_sb[0], n_heads=n_heads, head_dim=head_dim)
    glu = glu.reshape(bs, ts, d)
    cb = _conv_branch(glu, state_conv[0], *conv_args, tt=ts, nb=8, first_has_no_history=False, out_dtype=F32)
    x2 = _mix(x1, oa.reshape(n, d), cb.reshape(n, d), ga, gb, wa, wb, wo, *ln2, alpha=alpha, tm=n)
    y_sample = _ffn_ln(x2, w2u, w2d, *ln3, alpha=alpha, tm=n).reshape(bs, ts, d)
    k_sample = k.reshape(1, bs, ts, n_heads, head_dim)
    v_sample = v.reshape(1, bs, ts, n_heads, head_dim)
    conv_sample = jnp.concatenate([state_conv[0], glu], axis=1)[:, ts:, :][None]

    return y_prompt, y_sample, k_prompt, v_prompt, conv_prompt, k_sample, v_sample, conv_sample
```

```python
import functools
import math

import jax
import jax.numpy as jnp
from jax import lax
from jax.experimental import pallas as pl
from jax.experimental.pallas import tpu as pltpu

F32 = jnp.float32
BF16 = jnp.bfloat16

LN_EPS = 1e-5
MASKED_SCORE = -1e30
VMEM_LIMIT_BYTES = 56 * 1024 * 1024
LANES = 128
SUBLANES = 8
BF16_ROWS = 16
ATT_BLOCK = 256
CONV_HIST_PAD = 32


def _params(*sem):
    return pltpu.CompilerParams(dimension_semantics=sem, vmem_limit_bytes=VMEM_LIMIT_BYTES)


def _layer_norm(y, g, b):
    mu = jnp.mean(y, axis=-1, keepdims=True)
    d = y - mu
    var = jnp.mean(d * d, axis=-1, keepdims=True)
    return d * lax.rsqrt(var + LN_EPS) * g + b


def _sigmoid(x):
    return 1.0 / (1.0 + jnp.exp(-x))


def _softplus(z):
    return jnp.maximum(z, 0.0) + jnp.log(1.0 + jnp.exp(-jnp.abs(z)))


def _dot(a, b):
    return jnp.dot(a, b, preferred_element_type=F32)


def _dot_nt(a, b):
    return lax.dot_general(a, b, (((1,), (1,)), ((), ())), preferred_element_type=F32)


def _split_bf16(x):
    hi = x.astype(BF16)
    lo = (x - hi.astype(F32)).astype(BF16)
    return hi, lo


def _ffn_ln_kernel(x_ref, wg_ref, wu_ref, wd_ref, g_ref, b_ref, o_ref, *, alpha):
    x = x_ref[...]
    xb = x.astype(BF16)
    hg = _dot(xb, wg_ref[...])
    hu = _dot(xb, wu_ref[...])
    act = (hg * _sigmoid(hg) * hu).astype(BF16)
    y = alpha * x + 0.5 * _dot(act, wd_ref[...])
    o_ref[...] = _layer_norm(y, g_ref[...], b_ref[...])


def _resident(shape, index_map):
    return pl.BlockSpec(shape, index_map, pipeline_mode=pl.Buffered(1))


def _ffn_ln(x, w_up, w_down, g, b, *, alpha, tm):
    n, d = x.shape
    f = w_down.shape[0]
    assert f % LANES == 0 and n % tm == 0
    return pl.pallas_call(
        functools.partial(_ffn_ln_kernel, alpha=alpha),
        grid=(n // tm,),
        in_specs=[
            pl.BlockSpec((tm, d), lambda i: (i, 0)),
            _resident((d, f), lambda i: (0, 0)),
            _resident((d, f), lambda i: (0, 1)),
            _resident((f, d), lambda i: (0, 0)),
            pl.BlockSpec((1, d), lambda i: (0, 0)),
            pl.BlockSpec((1, d), lambda i: (0, 0)),
        ],
        out_specs=pl.BlockSpec((tm, d), lambda i: (i, 0)),
        out_shape=jax.ShapeDtypeStruct((n, d), F32),
        compiler_params=_params("parallel"),
        name="ffn_ln",
    )(x, w_up, w_up, w_down, g, b)


def _qkv_rows_kernel(x_ref, w_ref, q_ref, k_ref, v_ref, *, q_scale):
    d = x_ref.shape[1]
    h = _dot(x_ref[...].astype(BF16), w_ref[...])
    q_ref[...] = h[:, :d] * q_scale
    k_ref[...] = h[:, d:2 * d]
    v_ref[...] = h[:, 2 * d:]


def _qkv_rows(x, w, *, q_scale, tm):
    n, d = x.shape
    row = pl.BlockSpec((tm, d), lambda i: (i, 0))
    return pl.pallas_call(
        functools.partial(_qkv_rows_kernel, q_scale=q_scale),
        grid=(n // tm,),
        in_specs=[row, pl.BlockSpec((d, 3 * d), lambda i: (0, 0))],
        out_specs=[row] * 3,
        out_shape=[jax.ShapeDtypeStruct((n, d), F32)] * 3,
        compiler_params=_params("parallel"),
        name="qkv_rows",
    )(x, w)


def _qkv_cols_kernel(x_ref, wq_ref, wkt_ref, wvt_ref, q_ref, kt_ref, vt_ref, ktb_ref, vtb_ref, *, q_scale):
    xb = x_ref[...].astype(BF16)
    q_ref[...] = (_dot(xb, wq_ref[...]) * q_scale).astype(BF16)
    kt = _dot_nt(wkt_ref[...], xb)
    vt = _dot_nt(wvt_ref[...], xb)
    kt_ref[0] = kt
    vt_ref[0] = vt
    ktb_ref[0, 0] = kt.astype(BF16)
    vtb_ref[0, 0] = vt.astype(BF16)


def _qkv_cols(x, wq, wkt, wvt, *, batch, q_scale, tm):
    n, d = x.shape
    t = n // batch
    nt = t // tm
    mat = pl.BlockSpec((d, d), lambda b, i: (0, 0))
    return pl.pallas_call(
        functools.partial(_qkv_cols_kernel, q_scale=q_scale),
        grid=(batch, nt),
        in_specs=[pl.BlockSpec((tm, d), lambda b, i: (b * nt + i, 0)), mat, mat, mat],
        out_specs=[
            pl.BlockSpec((tm, d), lambda b, i: (b * nt + i, 0)),
            pl.BlockSpec((1, d, tm), lambda b, i: (b, 0, i)),
            pl.BlockSpec((1, d, tm), lambda b, i: (b, 0, i)),
            pl.BlockSpec((1, 1, d, tm), lambda b, i: (b, i, 0, 0)),
            pl.BlockSpec((1, 1, d, tm), lambda b, i: (b, i, 0, 0)),
        ],
        out_shape=[
            jax.ShapeDtypeStruct((n, d), BF16),
            jax.ShapeDtypeStruct((batch, d, t), F32),
            jax.ShapeDtypeStruct((batch, d, t), F32),
            jax.ShapeDtypeStruct((batch, nt, d, tm), BF16),
            jax.ShapeDtypeStruct((batch, nt, d, tm), BF16),
        ],
        compiler_params=_params("parallel", "parallel"),
        name="qkv_cols",
    )(x, wq, wkt, wvt)


def _glu_gate_kernel(x_ref, wc_ref, wg_ref, bg_ref, glu_ref, ga_ref, gb_ref):
    d = x_ref.shape[1]
    xb = x_ref[...].astype(BF16)
    c = _dot(xb, wc_ref[...])
    glu_ref[...] = c[:, :d] * _sigmoid(c[:, d:])
    gates = _sigmoid(_dot(xb, wg_ref[...]) + bg_ref[...])
    ga_ref[...] = gates[:, :d]
    gb_ref[...] = gates[:, d:]


def _glu_gate_proj(x, w_conv, w_gate, b_gate, *, tm):
    n, d = x.shape
    row = pl.BlockSpec((tm, d), lambda i: (i, 0))
    wide = pl.BlockSpec((d, 2 * d), lambda i: (0, 0))
    return pl.pallas_call(
        _glu_gate_kernel,
        grid=(n // tm,),
        in_specs=[row, wide, wide, pl.BlockSpec((1, 2 * d), lambda i: (0, 0))],
        out_specs=[row] * 3,
        out_shape=[jax.ShapeDtypeStruct((n, d), F32)] * 3,
        compiler_params=_params("parallel"),
        name="glu_gate_proj",
    )(x, w_conv, w_gate, b_gate)


def _softplus_n(z):
    neg_abs = lax.bitcast_convert_type(lax.bitcast_convert_type(z, jnp.uint32) | jnp.uint32(1 << 31), F32)
    return jnp.maximum(z, 0.0) + jnp.log(1.0 + jnp.exp(neg_abs))


def _attn_prompt_kernel(bias_ref, q_ref, kt_ref, vt_ref, tri_ref, o_ref,
                        y0_ref, h0_ref, y1_ref, h1_ref, acc_ref, carry_ref, *, head_dim):
    blk = ATT_BLOCK
    qblk = 2 * blk
    n_qblocks = q_ref.shape[1] // qblk
    pair = pl.program_id(1)
    heads = range(LANES // head_dim)
    lane = lax.broadcasted_iota(jnp.int32, (1, LANES), 1)
    row = lax.broadcasted_iota(jnp.int32, (qblk, blk), 0)
    col = lax.broadcasted_iota(jnp.int32, (qblk, blk), 1)
    causal = col < row
    tri2 = tri_ref[...]
    in_head = [(lane // head_dim) == hh for hh in heads]
    tile0 = (y0_ref, h0_ref)
    tile1 = (y1_ref, h1_ref)
    brow = lax.broadcasted_iota(jnp.int32, (LANES, blk), 0)
    bias_rows = []
    for hh in heads:
        bias = jnp.full((LANES, blk), bias_ref[pair * len(heads) + hh], F32)
        hi = bias.astype(BF16).astype(F32)
        bias_rows.append(jnp.where(brow == 0, hi, jnp.where(brow == 1, bias - hi, 0.0)).astype(BF16))
    ones = jnp.where(lax.broadcasted_iota(jnp.int32, (qblk, LANES), 1) < 2, 1.0, 0.0).astype(BF16)

    def scores(qh, kt, mask):
        ys = [_dot(qh[hh], jnp.concatenate([kt, bias_rows[hh]], axis=0)) for hh in heads]
        if mask is not None:
            ys = [jnp.where(mask, y, MASKED_SCORE) for y in ys]
        sps = [_softplus_n(y) for y in ys]
        return ys, [jnp.concatenate(_split_bf16(sp), axis=1) for sp in sps]

    def weighted_values(ys, hilos, vt):
        incls = [_dot(hl, tri2) for hl in hilos]
        ps = [jnp.exp(y - incl).astype(BF16) for y, incl in zip(ys, incls)]
        return [_dot_nt(p, vt) for p in ps], [incl[:, :1] for incl in incls]

    def scores_to(tile, qh, kt, mask):
        for hh, (y, hl) in enumerate(zip(*scores(qh, kt, mask))):
            tile[0][hh], tile[1][hh] = y, hl

    def accumulate_from(tile, vt, slot):
        pvs, rss = weighted_values([tile[0][hh] for hh in heads], [tile[1][hh] for hh in heads], vt)
        for hh in heads:
            carry = carry_ref[slot, hh]
            acc_ref[slot, hh] = acc_ref[slot, hh] + jnp.exp(-carry) * pvs[hh]
            carry_ref[slot, hh] = carry + rss[hh]

    tiles = (tile0, tile1)
    qhs, first = None, None
    for qi in range(n_qblocks + 1):
        slot, cur, other = qi % 2, tiles[qi % 2], tiles[1 - qi % 2]
        if qi < n_qblocks:
            q = q_ref[0, qi * qblk:(qi + 1) * qblk, :]
            qhs = [jnp.concatenate([jnp.where(m, q, jnp.zeros_like(q)), ones], axis=1) for m in in_head]
            scores_to(cur, qhs, kt_ref[0, 2 * qi], causal)
            first = scores([x[blk:] for x in qhs], kt_ref[0, 2 * qi + 1], causal[:blk])
        if qi > 0:
            accumulate_from(other, vt_ref[0, 0], 1 - slot)
            out = jnp.zeros((qblk, LANES), F32)
            for hh in heads:
                out = jnp.where(in_head[hh], acc_ref[1 - slot, hh], out)
            o_ref[0, (qi - 1) * qblk:qi * qblk, :] = out.astype(o_ref.dtype)
        if qi == n_qblocks:
            break
        pvs, rss = weighted_values(*first, vt_ref[0, 2 * qi + 1])
        for hh in heads:
            acc_ref[slot, hh] = jnp.concatenate([jnp.zeros_like(pvs[hh]), pvs[hh]], axis=0)
            carry_ref[slot, hh] = jnp.concatenate([jnp.zeros_like(rss[hh]), rss[hh]], axis=0)

        def k_block_body(i, _, qi=qi, qhs=qhs, slot=slot, cur=cur, other=other):
            kj = 2 * (qi - i)
            scores_to(other, qhs, kt_ref[0, kj - 1], None)
            accumulate_from(cur, vt_ref[0, kj], slot)
            scores_to(cur, qhs, kt_ref[0, kj - 2], None)
            accumulate_from(other, vt_ref[0, kj - 1], slot)
            return 0

        if qi > 0:
            lax.fori_loop(0, qi, k_block_body, 0)


def _attn_prompt(q, ktb, vtb, b_sb, *, head_dim):
    b, t, d = q.shape
    blk = ATT_BLOCK
    n_blocks = t // blk
    heads = LANES // head_dim
    assert ktb.shape == (b, n_blocks, d, blk) and d % LANES == 0 and LANES % head_dim == 0 and n_blocks % 2 == 0
    tri = (jnp.arange(blk)[:, None] >= jnp.arange(blk)[None, :]).astype(BF16)
    tri = jnp.concatenate([tri, tri], axis=0)
    seq = pl.BlockSpec((1, t, LANES), lambda i, p, bias: (i, 0, p))
    seq_t = pl.BlockSpec((1, n_blocks, LANES, blk), lambda i, p, bias: (i, 0, p, 0))
    grid_spec = pltpu.PrefetchScalarGridSpec(
        num_scalar_prefetch=1,
        grid=(b, d // LANES),
        in_specs=[seq, seq_t, seq_t, pl.BlockSpec((2 * blk, blk), lambda i, p, bias: (0, 0))],
        out_specs=seq,
        scratch_shapes=[
            pltpu.VMEM((heads, 2 * blk, blk), F32),
            pltpu.VMEM((heads, 2 * blk, 2 * blk), BF16),
        ] * 2 + [
            pltpu.VMEM((2, heads, 2 * blk, LANES), F32),
            pltpu.VMEM((2, heads, 2 * blk, 1), F32),
        ],
    )
    return pl.pallas_call(
        functools.partial(_attn_prompt_kernel, head_dim=head_dim),
        grid_spec=grid_spec,
        out_shape=jax.ShapeDtypeStruct((b, t, d), BF16),
        compiler_params=_params("parallel", "parallel"),
        name="attn_prompt",
    )(b_sb, q, ktb, vtb, tri)


def _attn_sample_kernel(pt_ref, bias_ref, tri_ref, sel_ref, q_ref, kn_ref, vn_ref, *refs,
                        n_pages, n_heads, head_dim):
    k_refs = refs[:n_pages]
    v_refs = refs[n_pages:2 * n_pages]
    o_ref, knp_ref, vnp_ref = refs[2 * n_pages:]
    tq, d = q_ref.shape[1], q_ref.shape[2]
    page = k_refs[0].shape[2]
    rows = bias_ref.shape[0]
    bias = bias_ref[...]
    tri2 = tri_ref[...]

    q = q_ref[0]
    r_idx = lax.broadcasted_iota(jnp.int32, (rows, d), 0)
    l_idx = lax.broadcasted_iota(jnp.int32, (rows, d), 1)
    q_rows = jnp.concatenate(
        [jnp.broadcast_to(q[i:i + 1, :], (n_heads, d)) for i in range(tq)]
        + [jnp.zeros((rows - tq * n_heads, d), F32)], axis=0)
    wq = jnp.where((l_idx // head_dim) == (r_idx % n_heads), q_rows, 0.0).astype(BF16)

    knp_ref[...] = jnp.zeros_like(knp_ref)
    vnp_ref[...] = jnp.zeros_like(vnp_ref)
    knp_ref[0:tq, :] = kn_ref[0]
    vnp_ref[0:tq, :] = vn_ref[0]
    knt = knp_ref[...].T
    vnt = vnp_ref[...].T
    i_idx = lax.broadcasted_iota(jnp.int32, (rows, page), 0) // n_heads
    j_idx = lax.broadcasted_iota(jnp.int32, (rows, page), 1)
    visible = j_idx < i_idx

    def scores(pg, carry):
        is_new = pg == n_pages
        kt = (knt if is_new else k_refs[pg][0]).astype(BF16)
        z = _dot(wq, kt) + bias
        sp = _softplus(z)
        spm = jnp.where(visible, sp, 0.0) if is_new else sp
        hilo = jnp.concatenate(_split_bf16(spm), axis=1)
        return (z - sp - carry, hilo), carry + jnp.sum(spm, axis=-1, keepdims=True)

    def weighted_values(pg, tile):
        is_new = pg == n_pages
        vt = (vnt if is_new else v_refs[pg][0]).astype(BF16)
        p = jnp.exp(tile[0] - _dot(tile[1], tri2))
        if is_new:
            p = jnp.where(visible, p, 0.0)
        return _dot_nt(vt, p.astype(BF16))

    acc = jnp.zeros((d, rows), F32)
    tile, carry = scores(n_pages, jnp.zeros((rows, 1), F32))
    for pg in range(n_pages, -1, -1):
        if pg > 0:
            nxt, carry = scores(pg - 1, carry)
        acc = acc + weighted_values(pg, tile)
        tile = nxt

    hr = lax.broadcasted_iota(jnp.int32, (d, rows), 0) // head_dim
    hc = lax.broadcasted_iota(jnp.int32, (d, rows), 1) % n_heads
    own = jnp.where(hr == hc, acc, 0.0).astype(BF16)
    o_ref[0] = _dot_nt(sel_ref[...], own)[0:tq, :]


def _attn_sample(q, k_new, v_new, cache_kt, cache_vt, page_table, b_sb, *, n_heads, head_dim):
    nb, tq, d = q.shape
    n_pages = page_table.shape[1]
    page = cache_kt.shape[2]
    used = tq * n_heads
    rows = -(-used // LANES) * LANES
    r = jnp.arange(rows)
    bias = jnp.where(r < used, b_sb.astype(F32)[r % n_heads], 0.0)
    bias = jnp.broadcast_to(bias[:, None], (rows, page))
    tri = (jnp.arange(page)[:, None] > jnp.arange(page)[None, :]).astype(BF16)
    tri2 = jnp.concatenate([tri, tri], axis=0)
    sel = ((r[None, :] // n_heads == jnp.arange(BF16_ROWS)[:, None]) & (r[None, :] < used)).astype(BF16)
    new = pl.BlockSpec((1, tq, d), lambda i, pt: (i, 0, 0))

    def const(a):
        return pl.BlockSpec(a.shape, lambda i, pt: (0,) * a.ndim)

    def page_spec(pg):
        return pl.BlockSpec((1, d, page), lambda i, pt: (pt[i * n_pages + pg], 0, 0))

    grid_spec = pltpu.PrefetchScalarGridSpec(
        num_scalar_prefetch=1,
        grid=(nb,),
        in_specs=[const(bias), const(tri2), const(sel), new, new, new]
        + [page_spec(pg) for pg in range(n_pages)] * 2,
        out_specs=new,
        scratch_shapes=[pltpu.VMEM((page, d), F32)] * 2,
    )
    return pl.pallas_call(
        functools.partial(_attn_sample_kernel, n_pages=n_pages, n_heads=n_heads, head_dim=head_dim),
        grid_spec=grid_spec,
        out_shape=jax.ShapeDtypeStruct((nb, tq, d), F32),
        compiler_params=_params("parallel"),
        name="attn_sample",
    )(page_table.reshape(-1), bias, tri2, sel, q, k_new, v_new, *([cache_kt] * n_pages), *([cache_vt] * n_pages))


def _conv_kernel(hist_ref, cur_ref, w_ref, b_ref, g_ref, be_ref, o_ref, ext_ref, *, width, first_has_no_history):
    nb, tt, c = cur_ref.shape
    pad = ext_ref.shape[1] - tt
    n_hist = width - 1
    hist = hist_ref[...]
    if first_has_no_history:
        hist = jnp.where(pl.program_id(1) == 0, 0.0, hist)
    ext_ref[:, pad - n_hist:pad, :] = hist[:, hist.shape[1] - n_hist:, :]
    ext_ref[:, pad:, :] = cur_ref[...]
    rows = min(tt, 64)
    first = pad - n_hist

    def taps(bi, r0, c0):
        if rows % SUBLANES:
            acc = jnp.zeros((rows, LANES), F32)
            for w in range(width):
                start = first + r0 + w
                acc = acc + ext_ref[bi, start:start + rows, c0:c0 + LANES] * w_ref[w:w + 1, c0:c0 + LANES]
            return acc
        e = ext_ref[bi, r0:r0 + rows + pad, c0:c0 + LANES]
        y = None
        for r in range(SUBLANES):
            n = rows if r == 0 else rows + SUBLANES
            u = None
            for o in range(r, pad + 1, SUBLANES):
                if 0 <= o - first < width:
                    term = e[o - r:o - r + n] * w_ref[o - first:o - first + 1, c0:c0 + LANES]
                    u = term if u is None else u + term
            part = u[r:r + rows]
            y = part if y is None else y + part
        return y

    for bi in range(nb):
        for r0 in range(0, tt, rows):
            cols = [taps(bi, r0, c0) for c0 in range(0, c, LANES)]
            y = jnp.concatenate(cols, axis=1) + b_ref[...]
            y = _layer_norm(y, g_ref[...], be_ref[...])
            o_ref[bi, r0:r0 + rows, :] = (y * _sigmoid(y)).astype(o_ref.dtype)


def _conv_branch(glu, hist, w_dw, b_dw, g, be, *, tt, nb, first_has_no_history, out_dtype):
    b, t, c = glu.shape
    width = w_dw.shape[0]
    th = hist.shape[1]
    if first_has_no_history:
        hist_rows = CONV_HIST_PAD
        hist_spec = pl.BlockSpec((nb, hist_rows, c), lambda i, j: (i, jnp.maximum(j * (tt // hist_rows) - 1, 0), 0))
    else:
        hist_spec = pl.BlockSpec((nb, th, c), lambda i, j: (i, 0, 0))
    vec = pl.BlockSpec((1, c), lambda i, j: (0, 0))
    return pl.pallas_call(
        functools.partial(_conv_kernel, width=width, first_has_no_history=first_has_no_history),
        grid=(b // nb, t // tt),
        in_specs=[hist_spec, pl.BlockSpec((nb, tt, c), lambda i, j: (i, j, 0)),
                  pl.BlockSpec((width, c), lambda i, j: (0, 0)), vec, vec, vec],
        out_specs=pl.BlockSpec((nb, tt, c), lambda i, j: (i, j, 0)),
        out_shape=jax.ShapeDtypeStruct((b, t, c), out_dtype),
        scratch_shapes=[pltpu.VMEM((nb, CONV_HIST_PAD + tt, c), F32)],
        compiler_params=_params("parallel", "parallel"),
        name="conv_branch",
    )(hist, glu, w_dw, b_dw, g, be)


def _mix_kernel(x_ref, oa_ref, cb_ref, ga_ref, gb_ref, wa_ref, wb_ref, wo_ref, g_ref, b_ref, o_ref, *, alpha):
    ya = _dot(oa_ref[...].astype(BF16), wa_ref[...])
    yb = _dot(cb_ref[...].astype(BF16), wb_ref[...])
    m = (ga_ref[...] * ya + gb_ref[...] * yb).astype(BF16)
    y = alpha * x_ref[...] + _dot(m, wo_ref[...])
    o_ref[...] = _layer_norm(y, g_ref[...], b_ref[...])


def _mix(x1, oa, cb, ga, gb, wa, wb, wo, g, b, *, alpha, tm):
    n, d = x1.shape
    row = pl.BlockSpec((tm, d), lambda i: (i, 0))
    mat = pl.BlockSpec((d, d), lambda i: (0, 0))
    vec = pl.BlockSpec((1, d), lambda i: (0, 0))
    return pl.pallas_call(
        functools.partial(_mix_kernel, alpha=alpha),
        grid=(n // tm,),
        in_specs=[row] * 5 + [mat] * 3 + [vec] * 2,
        out_specs=row,
        out_shape=jax.ShapeDtypeStruct((n, d), F32),
        compiler_params=_params("parallel"),
        name="mix",
    )(x1, oa, cb, ga, gb, wa, wb, wo, g, b)


def kernel(x_prompt, x_sample, cache_k, cache_v, state_conv, page_table, w_ff1_up, w_ff1_down, ln1_g, ln1_b, w_in, b_sb, b_gate, w_dw, b_dw, ln_conv_g, ln_conv_b, w_proj_a, w_proj_b, w_out, ln2_g, ln2_b, w_ff2_up, w_ff2_down, ln3_g, ln3_b):
    depth = w_in.shape[0]
    assert depth == 1, "single-layer step"
    bp, tp, d = x_prompt.shape
    bs, ts, _ = x_sample.shape
    n_heads = b_sb.shape[1]
    head_dim = d // n_heads
    n_pool, page = cache_k.shape[1], cache_k.shape[2]
    alpha = (2.0 * depth) ** 0.25
    q_scale = head_dim ** -0.5
    assert math.frexp(q_scale)[0] == 0.5, "scaling q before its bf16 rounding must be exact"
    n_hist = w_dw.shape[1] - 1

    def vec(a):
        return a[0].reshape(1, -1).astype(F32)

    w1u, w1d = w_ff1_up[0].astype(BF16), w_ff1_down[0].astype(BF16)
    w2u, w2d = w_ff2_up[0].astype(BF16), w_ff2_down[0].astype(BF16)
    w_qkv = w_in[0, :, :3 * d].astype(BF16)
    w_conv = w_in[0, :, 3 * d:5 * d].astype(BF16)
    w_gate = w_in[0, :, 5 * d:].astype(BF16)
    wa, wb, wo = w_proj_a[0].astype(BF16), w_proj_b[0].astype(BF16), w_out[0].astype(BF16)
    ln1, ln2, ln3 = (vec(ln1_g), vec(ln1_b)), (vec(ln2_g), vec(ln2_b)), (vec(ln3_g), vec(ln3_b))
    conv_args = (w_dw[0].astype(F32), vec(b_dw), vec(ln_conv_g), vec(ln_conv_b))

    n = bp * tp
    tm = 512
    x1 = _ffn_ln(x_prompt.reshape(n, d), w1u, w1d, *ln1, alpha=alpha, tm=tm)
    q, kt, vt, ktb, vtb = _qkv_cols(x1, w_qkv[:, :d], w_qkv[:, d:2 * d].T, w_qkv[:, 2 * d:].T,
                                    batch=bp, q_scale=q_scale, tm=ATT_BLOCK)
    glu, ga, gb = _glu_gate_proj(x1, w_conv, w_gate, vec(b_gate), tm=tm)
    oa = _attn_prompt(q.reshape(bp, tp, d), ktb, vtb, b_sb[0].astype(F32), head_dim=head_dim)
    glu = glu.reshape(bp, tp, d)
    cb = _conv_branch(glu, glu, *conv_args, tt=256, nb=1, first_has_no_history=True, out_dtype=BF16)
    x2 = _mix(x1, oa.reshape(n, d), cb.reshape(n, d), ga, gb, wa, wb, wo, *ln2, alpha=alpha, tm=tm)
    y_prompt = _ffn_ln(x2, w2u, w2d, *ln3, alpha=alpha, tm=tm).reshape(bp, tp, d)
    k_prompt = kt.reshape(bp, n_heads, head_dim, tp).transpose(0, 3, 1, 2)[None]
    v_prompt = vt.reshape(bp, n_heads, head_dim, tp).transpose(0, 3, 1, 2)[None]
    conv_prompt = glu[:, tp - n_hist:, :][None]

    n = bs * ts
    x1 = _ffn_ln(x_sample.reshape(n, d), w1u, w1d, *ln1, alpha=alpha, tm=n)
    q, k, v = _qkv_rows(x1, w_qkv, q_scale=q_scale, tm=256)
    glu, ga, gb = _glu_gate_proj(x1, w_conv, w_gate, vec(b_gate), tm=256)
    ckt = cache_k[0].transpose(0, 2, 3, 1).reshape(n_pool, d, page)
    cvt = cache_v[0].transpose(0, 2, 3, 1).reshape(n_pool, d, page)
    oa = _attn_sample(q.reshape(bs, ts, d), k.reshape(bs, ts, d), v.reshape(bs, ts, d), ckt, cvt, page_table,
                      b_sb[0], n_heads=n_heads, head_dim=head_dim)
    glu = glu.reshape(bs, ts, d)
    cb = _conv_branch(glu, state_conv[0], *conv_args, tt=ts, nb=8, first_has_no_history=False, out_dtype=F32)
    x2 = _mix(x1, oa.reshape(n, d), cb.reshape(n, d), ga, gb, wa, wb, wo, *ln2, alpha=alpha, tm=n)
    y_sample = _ffn_ln(x2, w2u, w2d, *ln3, alpha=alpha, tm=n).reshape(bs, ts, d)
    k_sample = k.reshape(1, bs, ts, n_heads, head_dim)
    v_sample = v.reshape(1, bs, ts, n_heads, head_dim)
    conv_sample = jnp.concatenate([state_conv[0], glu], axis=1)[:, ts:, :][None]

    return y_prompt, y_sample, k_prompt, v_prompt, conv_prompt, k_sample, v_sample, conv_sample
```

```python
import functools
import math

import jax
import jax.numpy as jnp
from jax import lax
from jax.experimental import pallas as pl
from jax.experimental.pallas import tpu as pltpu

F32 = jnp.float32
BF16 = jnp.bfloat16

LN_EPS = 1e-5
MASKED_SCORE = -1e30
VMEM_LIMIT_BYTES = 56 * 1024 * 1024
LANES = 128
SUBLANES = 8
BF16_ROWS = 16
ATT_BLOCK = 256
CONV_HIST_PAD = 32


def _params(*sem):
    return pltpu.CompilerParams(dimension_semantics=sem, vmem_limit_bytes=VMEM_LIMIT_BYTES)


def _layer_norm(y, g, b):
    mu = jnp.mean(y, axis=-1, keepdims=True)
    d = y - mu
    var = jnp.mean(d * d, axis=-1, keepdims=True)
    return d * lax.rsqrt(var + LN_EPS) * g + b


def _sigmoid(x):
    return 1.0 / (1.0 + jnp.exp(-x))


def _softplus(z):
    return jnp.maximum(z, 0.0) + jnp.log(1.0 + jnp.exp(-jnp.abs(z)))


def _dot(a, b):
    return jnp.dot(a, b, preferred_element_type=F32)


def _dot_nt(a, b):
    return lax.dot_general(a, b, (((1,), (1,)), ((), ())), preferred_element_type=F32)


def _split_bf16(x):
    hi = x.astype(BF16)
    lo = (x - hi.astype(F32)).astype(BF16)
    return hi, lo


def _ffn_ln_kernel(x_ref, wg_ref, wu_ref, wd_ref, g_ref, b_ref, o_ref, *, alpha):
    x = x_ref[...]
    xb = x.astype(BF16)
    hg = _dot(xb, wg_ref[...])
    hu = _dot(xb, wu_ref[...])
    act = (hg * _sigmoid(hg) * hu).astype(BF16)
    y = alpha * x + 0.5 * _dot(act, wd_ref[...])
    o_ref[...] = _layer_norm(y, g_ref[...], b_ref[...])


def _resident(shape, index_map):
    return pl.BlockSpec(shape, index_map, pipeline_mode=pl.Buffered(1))


def _ffn_ln(x, w_up, w_down, g, b, *, alpha, tm):
    n, d = x.shape
    f = w_down.shape[0]
    assert f % LANES == 0 and n % tm == 0
    return pl.pallas_call(
        functools.partial(_ffn_ln_kernel, alpha=alpha),
        grid=(n // tm,),
        in_specs=[
            pl.BlockSpec((tm, d), lambda i: (i, 0)),
            _resident((d, f), lambda i: (0, 0)),
            _resident((d, f), lambda i: (0, 1)),
            _resident((f, d), lambda i: (0, 0)),
            pl.BlockSpec((1, d), lambda i: (0, 0)),
            pl.BlockSpec((1, d), lambda i: (0, 0)),
        ],
        out_specs=pl.BlockSpec((tm, d), lambda i: (i, 0)),
        out_shape=jax.ShapeDtypeStruct((n, d), F32),
        compiler_params=_params("parallel"),
        name="ffn_ln",
    )(x, w_up, w_up, w_down, g, b)


def _qkv_rows_kernel(x_ref, w_ref, q_ref, k_ref, v_ref, *, q_scale):
    d = x_ref.shape[1]
    h = _dot(x_ref[...].astype(BF16), w_ref[...])
    q_ref[...] = h[:, :d] * q_scale
    k_ref[...] = h[:, d:2 * d]
    v_ref[...] = h[:, 2 * d:]


def _qkv_rows(x, w, *, q_scale, tm):
    n, d = x.shape
    row = pl.BlockSpec((tm, d), lambda i: (i, 0))
    return pl.pallas_call(
        functools.partial(_qkv_rows_kernel, q_scale=q_scale),
        grid=(n // tm,),
        in_specs=[row, pl.BlockSpec((d, 3 * d), lambda i: (0, 0))],
        out_specs=[row] * 3,
        out_shape=[jax.ShapeDtypeStruct((n, d), F32)] * 3,
        compiler_params=_params("parallel"),
        name="qkv_rows",
    )(x, w)


def _qkv_cols_kernel(x_ref, wq_ref, wkt_ref, wvt_ref, q_ref, kt_ref, vt_ref, *, q_scale):
    xb = x_ref[...].astype(BF16)
    q_ref[...] = (_dot(xb, wq_ref[...]) * q_scale).astype(BF16)
    kt_ref[0] = _dot_nt(wkt_ref[...], xb)
    vt_ref[0] = _dot_nt(wvt_ref[...], xb)


def _qkv_cols(x, wq, wkt, wvt, *, batch, q_scale, tm):
    n, d = x.shape
    t = n // batch
    nt = t // tm
    mat = pl.BlockSpec((d, d), lambda b, i: (0, 0))
    return pl.pallas_call(
        functools.partial(_qkv_cols_kernel, q_scale=q_scale),
        grid=(batch, nt),
        in_specs=[pl.BlockSpec((tm, d), lambda b, i: (b * nt + i, 0)), mat, mat, mat],
        out_specs=[
            pl.BlockSpec((tm, d), lambda b, i: (b * nt + i, 0)),
            pl.BlockSpec((1, d, tm), lambda b, i: (b, 0, i)),
            pl.BlockSpec((1, d, tm), lambda b, i: (b, 0, i)),
        ],
        out_shape=[
            jax.ShapeDtypeStruct((n, d), BF16),
            jax.ShapeDtypeStruct((batch, d, t), F32),
            jax.ShapeDtypeStruct((batch, d, t), F32),
        ],
        compiler_params=_params("parallel", "parallel"),
        name="qkv_cols",
    )(x, wq, wkt, wvt)


def _glu_gate_kernel(x_ref, wc_ref, wg_ref, bg_ref, glu_ref, ga_ref, gb_ref):
    d = x_ref.shape[1]
    xb = x_ref[...].astype(BF16)
    c = _dot(xb, wc_ref[...])
    glu_ref[...] = c[:, :d] * _sigmoid(c[:, d:])
    gates = _sigmoid(_dot(xb, wg_ref[...]) + bg_ref[...])
    ga_ref[...] = gates[:, :d]
    gb_ref[...] = gates[:, d:]


def _glu_gate_proj(x, w_conv, w_gate, b_gate, *, tm):
    n, d = x.shape
    row = pl.BlockSpec((tm, d), lambda i: (i, 0))
    wide = pl.BlockSpec((d, 2 * d), lambda i: (0, 0))
    return pl.pallas_call(
        _glu_gate_kernel,
        grid=(n // tm,),
        in_specs=[row, wide, wide, pl.BlockSpec((1, 2 * d), lambda i: (0, 0))],
        out_specs=[row] * 3,
        out_shape=[jax.ShapeDtypeStruct((n, d), F32)] * 3,
        compiler_params=_params("parallel"),
        name="glu_gate_proj",
    )(x, w_conv, w_gate, b_gate)


def _softplus_n(z):
    neg_abs = lax.bitcast_convert_type(lax.bitcast_convert_type(z, jnp.uint32) | jnp.uint32(1 << 31), F32)
    return jnp.maximum(z, 0.0) + jnp.log(1.0 + jnp.exp(neg_abs))


def _attn_prompt_kernel(bias_ref, q_ref, ktf_ref, vtf_ref, tri_ref, o_ref,
                        y0_ref, h0_ref, y1_ref, h1_ref, acc_ref, carry_ref, kt_ref, vt_ref, *, head_dim):
    blk = ATT_BLOCK
    qblk = 2 * blk
    n_qblocks = q_ref.shape[1] // qblk
    for j in range(kt_ref.shape[0]):
        kt_ref[j] = ktf_ref[0, :, j * blk:(j + 1) * blk].astype(BF16)
        vt_ref[j] = vtf_ref[0, :, j * blk:(j + 1) * blk].astype(BF16)
    pair = pl.program_id(1)
    heads = range(LANES // head_dim)
    lane = lax.broadcasted_iota(jnp.int32, (1, LANES), 1)
    row = lax.broadcasted_iota(jnp.int32, (qblk, blk), 0)
    col = lax.broadcasted_iota(jnp.int32, (qblk, blk), 1)
    causal = col < row
    tri2 = tri_ref[...]
    in_head = [(lane // head_dim) == hh for hh in heads]
    tile0 = (y0_ref, h0_ref)
    tile1 = (y1_ref, h1_ref)
    brow = lax.broadcasted_iota(jnp.int32, (LANES, blk), 0)
    bias_rows = []
    for hh in heads:
        bias = jnp.full((LANES, blk), bias_ref[pair * len(heads) + hh], F32)
        hi = bias.astype(BF16).astype(F32)
        bias_rows.append(jnp.where(brow == 0, hi, jnp.where(brow == 1, bias - hi, 0.0)).astype(BF16))
    ones = jnp.where(lax.broadcasted_iota(jnp.int32, (qblk, LANES), 1) < 2, 1.0, 0.0).astype(BF16)

    def scores(qh, kt, mask):
        ys = [_dot(qh[hh], jnp.concatenate([kt, bias_rows[hh]], axis=0)) for hh in heads]
        if mask is not None:
            ys = [jnp.where(mask, y, MASKED_SCORE) for y in ys]
        sps = [_softplus_n(y) for y in ys]
        return ys, [jnp.concatenate(_split_bf16(sp), axis=1) for sp in sps]

    def weighted_values(ys, hilos, vt):
        incls = [_dot(hl, tri2) for hl in hilos]
        ps = [jnp.exp(y - incl).astype(BF16) for y, incl in zip(ys, incls)]
        return [_dot_nt(p, vt) for p in ps], [incl[:, :1] for incl in incls]

    def scores_to(tile, qh, kt, mask):
        for hh, (y, hl) in enumerate(zip(*scores(qh, kt, mask))):
            tile[0][hh], tile[1][hh] = y, hl

    def accumulate_from(tile, vt, slot):
        pvs, rss = weighted_values([tile[0][hh] for hh in heads], [tile[1][hh] for hh in heads], vt)
        for hh in heads:
            carry = carry_ref[slot, hh]
            acc_ref[slot, hh] = acc_ref[slot, hh] + jnp.exp(-carry) * pvs[hh]
            carry_ref[slot, hh] = carry + rss[hh]

    tiles = (tile0, tile1)
    qhs, first = None, None
    for qi in range(n_qblocks + 1):
        slot, cur, other = qi % 2, tiles[qi % 2], tiles[1 - qi % 2]
        if qi < n_qblocks:
            q = q_ref[0, qi * qblk:(qi + 1) * qblk, :]
            qhs = [jnp.concatenate([jnp.where(m, q, jnp.zeros_like(q)), ones], axis=1) for m in in_head]
            scores_to(cur, qhs, kt_ref[2 * qi], causal)
            first = scores([x[blk:] for x in qhs], kt_ref[2 * qi + 1], causal[:blk])
        if qi > 0:
            accumulate_from(other, vt_ref[0], 1 - slot)
            out = jnp.zeros((qblk, LANES), F32)
            for hh in heads:
                out = jnp.where(in_head[hh], acc_ref[1 - slot, hh], out)
            o_ref[0, (qi - 1) * qblk:qi * qblk, :] = out.astype(o_ref.dtype)
        if qi == n_qblocks:
            break
        pvs, rss = weighted_values(*first, vt_ref[2 * qi + 1])
        for hh in heads:
            acc_ref[slot, hh] = jnp.concatenate([jnp.zeros_like(pvs[hh]), pvs[hh]], axis=0)
            carry_ref[slot, hh] = jnp.concatenate([jnp.zeros_like(rss[hh]), rss[hh]], axis=0)

        def k_block_body(i, _, qi=qi, qhs=qhs, slot=slot, cur=cur, other=other):
            kj = 2 * (qi - i)
            scores_to(other, qhs, kt_ref[kj - 1], None)
            accumulate_from(cur, vt_ref[kj], slot)
            scores_to(cur, qhs, kt_ref[kj - 2], None)
            accumulate_from(other, vt_ref[kj - 1], slot)
            return 0

        if qi > 0:
            lax.fori_loop(0, qi, k_block_body, 0)


def _attn_prompt(q, ktb, vtb, b_sb, *, head_dim):
    b, t, d = q.shape
    blk = ATT_BLOCK
    n_blocks = t // blk
    heads = LANES // head_dim
    assert ktb.shape == (b, d, t) and d % LANES == 0 and LANES % head_dim == 0 and n_blocks % 2 == 0
    tri = (jnp.arange(blk)[:, None] >= jnp.arange(blk)[None, :]).astype(BF16)
    tri = jnp.concatenate([tri, tri], axis=0)
    seq = pl.BlockSpec((1, t, LANES), lambda i, p, bias: (i, 0, p))
    seq_t = pl.BlockSpec((1, LANES, t), lambda i, p, bias: (i, p, 0))
    grid_spec = pltpu.PrefetchScalarGridSpec(
        num_scalar_prefetch=1,
        grid=(b, d // LANES),
        in_specs=[seq, seq_t, seq_t, pl.BlockSpec((2 * blk, blk), lambda i, p, bias: (0, 0))],
        out_specs=seq,
        scratch_shapes=[
            pltpu.VMEM((heads, 2 * blk, blk), F32),
            pltpu.VMEM((heads, 2 * blk, 2 * blk), BF16),
        ] * 2 + [
            pltpu.VMEM((2, heads, 2 * blk, LANES), F32),
            pltpu.VMEM((2, heads, 2 * blk, 1), F32),
            pltpu.VMEM((n_blocks, LANES, blk), BF16),
            pltpu.VMEM((n_blocks, LANES, blk), BF16),
        ],
    )
    return pl.pallas_call(
        functools.partial(_attn_prompt_kernel, head_dim=head_dim),
        grid_spec=grid_spec,
        out_shape=jax.ShapeDtypeStruct((b, t, d), BF16),
        compiler_params=_params("parallel", "parallel"),
        name="attn_prompt",
    )(b_sb, q, ktb, vtb, tri)


def _attn_sample_kernel(pt_ref, bias_ref, tri_ref, sel_ref, q_ref, kn_ref, vn_ref, *refs,
                        n_pages, n_heads, head_dim):
    k_refs = refs[:n_pages]
    v_refs = refs[n_pages:2 * n_pages]
    o_ref, knp_ref, vnp_ref = refs[2 * n_pages:]
    tq, d = q_ref.shape[1], q_ref.shape[2]
    page = k_refs[0].shape[2]
    rows = bias_ref.shape[0]
    bias = bias_ref[...]
    tri2 = tri_ref[...]

    q = q_ref[0]
    r_idx = lax.broadcasted_iota(jnp.int32, (rows, d), 0)
    l_idx = lax.broadcasted_iota(jnp.int32, (rows, d), 1)
    q_rows = jnp.concatenate(
        [jnp.broadcast_to(q[i:i + 1, :], (n_heads, d)) for i in range(tq)]
        + [jnp.zeros((rows - tq * n_heads, d), F32)], axis=0)
    wq = jnp.where((l_idx // head_dim) == (r_idx % n_heads), q_rows, 0.0).astype(BF16)

    knp_ref[...] = jnp.zeros_like(knp_ref)
    vnp_ref[...] = jnp.zeros_like(vnp_ref)
    knp_ref[0:tq, :] = kn_ref[0]
    vnp_ref[0:tq, :] = vn_ref[0]
    knt = knp_ref[...].T
    vnt = vnp_ref[...].T
    i_idx = lax.broadcasted_iota(jnp.int32, (rows, page), 0) // n_heads
    j_idx = lax.broadcasted_iota(jnp.int32, (rows, page), 1)
    visible = j_idx < i_idx

    def scores(pg, carry):
        is_new = pg == n_pages
        kt = (knt if is_new else k_refs[pg][0]).astype(BF16)
        z = _dot(wq, kt) + bias
        sp = _softplus(z)
        spm = jnp.where(visible, sp, 0.0) if is_new else sp
        hilo = jnp.concatenate(_split_bf16(spm), axis=1)
        return (z - sp - carry, hilo), carry + jnp.sum(spm, axis=-1, keepdims=True)

    def weighted_values(pg, tile):
        is_new = pg == n_pages
        vt = (vnt if is_new else v_refs[pg][0]).astype(BF16)
        p = jnp.exp(tile[0] - _dot(tile[1], tri2))
        if is_new:
            p = jnp.where(visible, p, 0.0)
        return _dot_nt(vt, p.astype(BF16))

    acc = jnp.zeros((d, rows), F32)
    tile, carry = scores(n_pages, jnp.zeros((rows, 1), F32))
    for pg in range(n_pages, -1, -1):
        if pg > 0:
            nxt, carry = scores(pg - 1, carry)
        acc = acc + weighted_values(pg, tile)
        tile = nxt

    hr = lax.broadcasted_iota(jnp.int32, (d, rows), 0) // head_dim
    hc = lax.broadcasted_iota(jnp.int32, (d, rows), 1) % n_heads
    own = jnp.where(hr == hc, acc, 0.0).astype(BF16)
    o_ref[0] = _dot_nt(sel_ref[...], own)[0:tq, :]


def _attn_sample(q, k_new, v_new, cache_kt, cache_vt, page_table, b_sb, *, n_heads, head_dim):
    nb, tq, d = q.shape
    n_pages = page_table.shape[1]
    page = cache_kt.shape[2]
    used = tq * n_heads
    rows = -(-used // LANES) * LANES
    r = jnp.arange(rows)
    bias = jnp.where(r < used, b_sb.astype(F32)[r % n_heads], 0.0)
    bias = jnp.broadcast_to(bias[:, None], (rows, page))
    tri = (jnp.arange(page)[:, None] > jnp.arange(page)[None, :]).astype(BF16)
    tri2 = jnp.concatenate([tri, tri], axis=0)
    sel = ((r[None, :] // n_heads == jnp.arange(BF16_ROWS)[:, None]) & (r[None, :] < used)).astype(BF16)
    new = pl.BlockSpec((1, tq, d), lambda i, pt: (i, 0, 0))

    def const(a):
        return pl.BlockSpec(a.shape, lambda i, pt: (0,) * a.ndim)

    def page_spec(pg):
        return pl.BlockSpec((1, d, page), lambda i, pt: (pt[i * n_pages + pg], 0, 0))

    grid_spec = pltpu.PrefetchScalarGridSpec(
        num_scalar_prefetch=1,
        grid=(nb,),
        in_specs=[const(bias), const(tri2), const(sel), new, new, new]
        + [page_spec(pg) for pg in range(n_pages)] * 2,
        out_specs=new,
        scratch_shapes=[pltpu.VMEM((page, d), F32)] * 2,
    )
    return pl.pallas_call(
        functools.partial(_attn_sample_kernel, n_pages=n_pages, n_heads=n_heads, head_dim=head_dim),
        grid_spec=grid_spec,
        out_shape=jax.ShapeDtypeStruct((nb, tq, d), F32),
        compiler_params=_params("parallel"),
        name="attn_sample",
    )(page_table.reshape(-1), bias, tri2, sel, q, k_new, v_new, *([cache_kt] * n_pages), *([cache_vt] * n_pages))


def _conv_kernel(hist_ref, cur_ref, w_ref, b_ref, g_ref, be_ref, o_ref, ext_ref, *, width, first_has_no_history):
    nb, tt, c = cur_ref.shape
    pad = ext_ref.shape[1] - tt
    n_hist = width - 1
    hist = hist_ref[...]
    if first_has_no_history:
        hist = jnp.where(pl.program_id(1) == 0, 0.0, hist)
    ext_ref[:, pad - n_hist:pad, :] = hist[:, hist.shape[1] - n_hist:, :]
    ext_ref[:, pad:, :] = cur_ref[...]
    rows = min(tt, 64)
    first = pad - n_hist

    def taps(bi, r0, c0):
        if rows % SUBLANES:
            acc = jnp.zeros((rows, LANES), F32)
            for w in range(width):
                start = first + r0 + w
                acc = acc + ext_ref[bi, start:start + rows, c0:c0 + LANES] * w_ref[w:w + 1, c0:c0 + LANES]
            return acc
        e = ext_ref[bi, r0:r0 + rows + pad, c0:c0 + LANES]
        y = None
        for r in range(SUBLANES):
            n = rows if r == 0 else rows + SUBLANES
            u = None
            for o in range(r, pad + 1, SUBLANES):
                if 0 <= o - first < width:
                    term = e[o - r:o - r + n] * w_ref[o - first:o - first + 1, c0:c0 + LANES]
                    u = term if u is None else u + term
            part = u[r:r + rows]
            y = part if y is None else y + part
        return y

    for bi in range(nb):
        for r0 in range(0, tt, rows):
            cols = [taps(bi, r0, c0) for c0 in range(0, c, LANES)]
            y = jnp.concatenate(cols, axis=1) + b_ref[...]
            y = _layer_norm(y, g_ref[...], be_ref[...])
            o_ref[bi, r0:r0 + rows, :] = (y * _sigmoid(y)).astype(o_ref.dtype)


def _conv_branch(glu, hist, w_dw, b_dw, g, be, *, tt, nb, first_has_no_history, out_dtype):
    b, t, c = glu.shape
    width = w_dw.shape[0]
    th = hist.shape[1]
    if first_has_no_history:
        hist_rows = CONV_HIST_PAD
        hist_spec = pl.BlockSpec((nb, hist_rows, c), lambda i, j: (i, jnp.maximum(j * (tt // hist_rows) - 1, 0), 0))
    else:
        hist_spec = pl.BlockSpec((nb, th, c), lambda i, j: (i, 0, 0))
    vec = pl.BlockSpec((1, c), lambda i, j: (0, 0))
    return pl.pallas_call(
        functools.partial(_conv_kernel, width=width, first_has_no_history=first_has_no_history),
        grid=(b // nb, t // tt),
        in_specs=[hist_spec, pl.BlockSpec((nb, tt, c), lambda i, j: (i, j, 0)),
                  pl.BlockSpec((width, c), lambda i, j: (0, 0)), vec, vec, vec],
        out_specs=pl.BlockSpec((nb, tt, c), lambda i, j: (i, j, 0)),
        out_shape=jax.ShapeDtypeStruct((b, t, c), out_dtype),
        scratch_shapes=[pltpu.VMEM((nb, CONV_HIST_PAD + tt, c), F32)],
        compiler_params=_params("parallel", "parallel"),
        name="conv_branch",
    )(hist, glu, w_dw, b_dw, g, be)


def _mix_kernel(x_ref, oa_ref, cb_ref, ga_ref, gb_ref, wa_ref, wb_ref, wo_ref, g_ref, b_ref, o_ref, *, alpha):
    ya = _dot(oa_ref[...].astype(BF16), wa_ref[...])
    yb = _dot(cb_ref[...].astype(BF16), wb_ref[...])
    m = (ga_ref[...] * ya + gb_ref[...] * yb).astype(BF16)
    y = alpha * x_ref[...] + _dot(m, wo_ref[...])
    o_ref[...] = _layer_norm(y, g_ref[...], b_ref[...])


def _mix(x1, oa, cb, ga, gb, wa, wb, wo, g, b, *, alpha, tm):
    n, d = x1.shape
    row = pl.BlockSpec((tm, d), lambda i: (i, 0))
    mat = pl.BlockSpec((d, d), lambda i: (0, 0))
    vec = pl.BlockSpec((1, d), lambda i: (0, 0))
    return pl.pallas_call(
        functools.partial(_mix_kernel, alpha=alpha),
        grid=(n // tm,),
        in_specs=[row] * 5 + [mat] * 3 + [vec] * 2,
        out_specs=row,
        out_shape=jax.ShapeDtypeStruct((n, d), F32),
        compiler_params=_params("parallel"),
        name="mix",
    )(x1, oa, cb, ga, gb, wa, wb, wo, g, b)


def kernel(x_prompt, x_sample, cache_k, cache_v, state_conv, page_table, w_ff1_up, w_ff1_down, ln1_g, ln1_b, w_in, b_sb, b_gate, w_dw, b_dw, ln_conv_g, ln_conv_b, w_proj_a, w_proj_b, w_out, ln2_g, ln2_b, w_ff2_up, w_ff2_down, ln3_g, ln3_b):
    depth = w_in.shape[0]
    assert depth == 1, "single-layer step"
    bp, tp, d = x_prompt.shape
    bs, ts, _ = x_sample.shape
    n_heads = b_sb.shape[1]
    head_dim = d // n_heads
    n_pool, page = cache_k.shape[1], cache_k.shape[2]
    alpha = (2.0 * depth) ** 0.25
    q_scale = head_dim ** -0.5
    assert math.frexp(q_scale)[0] == 0.5, "scaling q before its bf16 rounding must be exact"
    n_hist = w_dw.shape[1] - 1

    def vec(a):
        return a[0].reshape(1, -1).astype(F32)

    w1u, w1d = w_ff1_up[0].astype(BF16), w_ff1_down[0].astype(BF16)
    w2u, w2d = w_ff2_up[0].astype(BF16), w_ff2_down[0].astype(BF16)
    w_qkv = w_in[0, :, :3 * d].astype(BF16)
    w_conv = w_in[0, :, 3 * d:5 * d].astype(BF16)
    w_gate = w_in[0, :, 5 * d:].astype(BF16)
    wa, wb, wo = w_proj_a[0].astype(BF16), w_proj_b[0].astype(BF16), w_out[0].astype(BF16)
    ln1, ln2, ln3 = (vec(ln1_g), vec(ln1_b)), (vec(ln2_g), vec(ln2_b)), (vec(ln3_g), vec(ln3_b))
    conv_args = (w_dw[0].astype(F32), vec(b_dw), vec(ln_conv_g), vec(ln_conv_b))

    n = bp * tp
    tm = 512
    x1 = _ffn_ln(x_prompt.reshape(n, d), w1u, w1d, *ln1, alpha=alpha, tm=tm)
    q, kt, vt = _qkv_cols(x1, w_qkv[:, :d], w_qkv[:, d:2 * d].T, w_qkv[:, 2 * d:].T,
                                    batch=bp, q_scale=q_scale, tm=ATT_BLOCK)
    glu, ga, gb = _glu_gate_proj(x1, w_conv, w_gate, vec(b_gate), tm=tm)
    oa = _attn_prompt(q.reshape(bp, tp, d), kt, vt, b_sb[0].astype(F32), head_dim=head_dim)
    glu = glu.reshape(bp, tp, d)
    cb = _conv_branch(glu, glu, *conv_args, tt=256, nb=1, first_has_no_history=True, out_dtype=BF16)
    x2 = _mix(x1, oa.reshape(n, d), cb.reshape(n, d), ga, gb, wa, wb, wo, *ln2, alpha=alpha, tm=tm)
    y_prompt = _ffn_ln(x2, w2u, w2d, *ln3, alpha=alpha, tm=tm).reshape(bp, tp, d)
    k_prompt = kt.reshape(bp, n_heads, head_dim, tp).transpose(0, 3, 1, 2)[None]
    v_prompt = vt.reshape(bp, n_heads, head_dim, tp).transpose(0, 3, 1, 2)[None]
    conv_prompt = glu[:, tp - n_hist:, :][None]

    n = bs * ts
    x1 = _ffn_ln(x_sample.reshape(n, d), w1u, w1d, *ln1, alpha=alpha, tm=n)
    q, k, v = _qkv_rows(x1, w_qkv, q_scale=q_scale, tm=256)
    glu, ga, gb = _glu_gate_proj(x1, w_conv, w_gate, vec(b_gate), tm=256)
    ckt = cache_k[0].transpose(0, 2, 3, 1).reshape(n_pool, d, page)
    cvt = cache_v[0].transpose(0, 2, 3, 1).reshape(n_pool, d, page)
    oa = _attn_sample(q.reshape(bs, ts, d), k.reshape(bs, ts, d), v.reshape(bs, ts, d), ckt, cvt, page_table,
                      b_sb[0], n_heads=n_heads, head_dim=head_dim)
    glu = glu.reshape(bs, ts, d)
    cb = _conv_branch(glu, state_conv[0], *conv_args, tt=ts, nb=8, first_has_no_history=False, out_dtype=F32)
    x2 = _mix(x1, oa.reshape(n, d), cb.reshape(n, d), ga, gb, wa, wb, wo, *ln2, alpha=alpha, tm=n)
    y_sample = _ffn_ln(x2, w2u, w2d, *ln3, alpha=alpha, tm=n).reshape(bs, ts, d)
    k_sample = k.reshape(1, bs, ts, n_heads, head_dim)
    v_sample = v.reshape(1, bs, ts, n_heads, head_dim)
    conv_sample = jnp.concatenate([state_conv[0], glu], axis=1)[:, ts:, :][None]

    return y_prompt, y_sample, k_prompt, v_prompt, conv_prompt, k_sample, v_sample, conv_sample
```
